```python
import math
import jax
import jax.numpy as jnp
from jax import lax
import numpy as np

D_MODEL = 1024
BATCH = 32
SEQ = 2048
DEPTH = 4

MIX_WIDTH = D_MODEL
GROUP_WIDTH = MIX_WIDTH // 4
HEAD_DIM = 64
NORM_EPS = 1e-6

SSD_HEADS = GROUP_WIDTH // HEAD_DIM
SSD_STATE = 64
SSD_BC_GROUPS = 2
SSD_CONV = 4
SSD_CHUNK = 128
SSD_BC = SSD_BC_GROUPS * SSD_STATE
SSD_XBC = GROUP_WIDTH + 2 * SSD_BC
SSD_COLS = SSD_XBC + GROUP_WIDTH + SSD_HEADS

ATT_HEADS = GROUP_WIDTH // HEAD_DIM
DILATED_PATTERNS = ((128, 1), (512, 4), (2048, 16))
ATT_BLOCK = 128
ATT_COLS = 3 * GROUP_WIDTH

RWKV_HEADS = GROUP_WIDTH // HEAD_DIM
RWKV_DECAY_RANK = 64
RWKV_A_RANK = 64
RWKV_GATE_RANK = 128
RWKV_DECAY_SCALE = 0.606531
RWKV_GN_EPS = 64e-5
RWKV_COLS = 3 * GROUP_WIDTH + RWKV_DECAY_RANK + RWKV_A_RANK + RWKV_GATE_RANK
RWKV_SPLITS = (GROUP_WIDTH, 2 * GROUP_WIDTH, 3 * GROUP_WIDTH,
               3 * GROUP_WIDTH + RWKV_DECAY_RANK,
               3 * GROUP_WIDTH + RWKV_DECAY_RANK + RWKV_A_RANK)

S5_GROUP = 16
S5_GROUPS = GROUP_WIDTH // S5_GROUP
S5_STATE = 64
S5_COLS = GROUP_WIDTH

MIX_COLS = SSD_COLS + ATT_COLS + RWKV_COLS + S5_COLS
MIX_SPLITS = (SSD_XBC, SSD_XBC + GROUP_WIDTH, SSD_COLS, SSD_COLS + ATT_COLS,
              SSD_COLS + ATT_COLS + RWKV_COLS)

D_FF = 2816
N_EXPERTS = 8
TOP_K = 2
MOE_BLOCK = 256
N_DENSE = (DEPTH + 1) // 2
N_MOE = DEPTH // 2

kernel_name = 'hybrid_ssd_dilatedattn_rwkv7_s5_moe_trunk'


def rms_norm(x, gain, eps=NORM_EPS):
    xf = x.astype(jnp.float32)
    y = xf * lax.rsqrt(jnp.mean(xf * xf, axis=-1, keepdims=True) + eps)
    return (y * gain.astype(jnp.float32)).astype(x.dtype)


def swiglu(h, w_gate, w_up, w_down):
    return (jax.nn.silu(h @ w_gate) * (h @ w_up)) @ w_down


def causal_depthwise_conv(u, w, b):
    ch = u.shape[-1]
    out = lax.conv_general_dilated(
        u, w[:, None, :].astype(u.dtype), window_strides=(1,),
        padding=[(w.shape[0] - 1, 0)], dimension_numbers=('NWC', 'WIO', 'NWC'),
        feature_group_count=ch)
    return out + b.astype(u.dtype)


def segsum(a):
    t = a.shape[-1]
    cs = jnp.cumsum(a, axis=-1)
    diff = cs[..., :, None] - cs[..., None, :]
    return jnp.where(jnp.tril(jnp.ones((t, t), dtype=bool)), diff, -jnp.inf)


def ssd_chunked(x, a, b_mat, c_mat):
    bsz, seq, heads, hd = x.shape
    n_chunks = seq // SSD_CHUNK
    x = x.reshape(bsz, n_chunks, SSD_CHUNK, heads, hd)
    b_mat = b_mat.reshape(bsz, n_chunks, SSD_CHUNK, heads, -1)
    c_mat = c_mat.reshape(bsz, n_chunks, SSD_CHUNK, heads, -1)
    a = a.reshape(bsz, n_chunks, SSD_CHUNK, heads).transpose(0, 3, 1, 2)
    a_cs = jnp.cumsum(a, axis=-1)
    decay = jnp.exp(segsum(a))
    scores = jnp.einsum('bclhn,bcshn->bhcls', c_mat, b_mat) * decay
    y_diag = jnp.einsum('bhcls,bcshp->bclhp', scores, x)
    decay_to_end = jnp.exp(a_cs[..., -1:] - a_cs)
    states = jnp.einsum('bclhn,bhcl,bclhp->bchpn', b_mat, decay_to_end, x)
    states = jnp.concatenate([jnp.zeros_like(states[:, :1]), states], axis=1)
    chunk_decay = jnp.exp(segsum(jnp.pad(a_cs[..., -1], ((0, 0), (0, 0), (1, 0)))))
    states = jnp.einsum('bhzc,bchpn->bzhpn', chunk_decay, states)[:, :-1]
    y_off = jnp.einsum('bclhn,bchpn,bhcl->bclhp', c_mat, states, jnp.exp(a_cs))
    return (y_diag + y_off).reshape(bsz, seq, heads, hd)


def ssd_mixer(xbc, z, dt_raw, conv_w, conv_b, dt_bias, a_log, d_skip, norm_gain):
    bsz, seq, _ = xbc.shape
    xbc = jax.nn.silu(causal_depthwise_conv(xbc, conv_w, conv_b)).astype(jnp.float32)
    xs, b_mat, c_mat = jnp.split(xbc, (GROUP_WIDTH, GROUP_WIDTH + SSD_BC), axis=-1)
    xs = xs.reshape(bsz, seq, SSD_HEADS, HEAD_DIM)
    heads_per_group = SSD_HEADS // SSD_BC_GROUPS
    b_mat = jnp.repeat(b_mat.reshape(bsz, seq, SSD_BC_GROUPS, SSD_STATE), heads_per_group, axis=2)
    c_mat = jnp.repeat(c_mat.reshape(bsz, seq, SSD_BC_GROUPS, SSD_STATE), heads_per_group, axis=2)
    dt = jax.nn.softplus(dt_raw.astype(jnp.float32) + dt_bias.astype(jnp.float32))
    a = -jnp.exp(a_log.astype(jnp.float32))
    y = ssd_chunked(xs * dt[..., None], dt * a, b_mat, c_mat)
    y = y + d_skip.astype(jnp.float32)[:, None] * xs
    y = y.reshape(bsz, seq, GROUP_WIDTH) * jax.nn.silu(z.astype(jnp.float32))
    return rms_norm(y, norm_gain)


def dilated_branch(q, k, v, slopes, window, dilation):
    bsz, seq, heads, hd = q.shape
    n_cls = seq // dilation
    span = window // dilation
    n_blk = -(-n_cls // ATT_BLOCK)
    pad = n_blk * ATT_BLOCK - n_cls

    def to_blocks(t):
        t = t.reshape(bsz, n_cls, dilation, heads, hd).transpose(0, 2, 1, 3, 4)
        t = jnp.pad(t, ((0, 0), (0, 0), (0, pad), (0, 0), (0, 0)))
        return t.reshape(bsz, dilation, n_blk, ATT_BLOCK, heads, hd)

    def with_prev(t):
        prev = jnp.pad(t, ((0, 0), (0, 0), (1, 0), (0, 0), (0, 0), (0, 0)))[:, :, :-1]
        return jnp.concatenate([prev, t], axis=3)

    def from_blocks(t):
        t = t.reshape(bsz, dilation, n_blk * ATT_BLOCK, *t.shape[4:])[:, :, :n_cls]
        t = jnp.swapaxes(t, 1, 2)
        return t.reshape(bsz, seq, *t.shape[3:])

    qb = to_blocks(q)
    kb = with_prev(to_blocks(k))
    vb = with_prev(to_blocks(v))
    scores = jnp.einsum('bdnqhe,bdnkhe->bdnhqk', qb, kb) / math.sqrt(hd)
    q_idx = jnp.arange(ATT_BLOCK)[:, None]
    k_idx = jnp.arange(2 * ATT_BLOCK)[None, :] - ATT_BLOCK
    rel = q_idx - k_idx
    key_pos = jnp.arange(n_blk)[:, None] * ATT_BLOCK + k_idx
    valid = ((rel >= 0) & (rel <= span))[None] & (key_pos >= 0)[:, None, :]
    alibi = -slopes[:, None, None] * (rel * dilation).astype(jnp.float32)[None]
    scores = scores + alibi[None, None, None]
    scores = jnp.where(valid[None, None, :, None], scores, -jnp.inf)
    row_max = jnp.max(scores, axis=-1)
    probs = jnp.exp(scores - row_max[..., None])
    denom = jnp.sum(probs, axis=-1)
    row_max = row_max.transpose(0, 1, 2, 4, 3)
    denom = denom.transpose(0, 1, 2, 4, 3)
    out = jnp.einsum('bdnhqk,bdnkhe->bdnqhe', probs, vb) / denom[..., None]
    return from_blocks(out), from_blocks(row_max), from_blocks(denom)


def dilated_attention(qkv, q_gain, k_gain, out_gain):
    bsz, seq, _ = qkv.shape
    q, k, v = jnp.split(qkv.astype(jnp.float32), 3, axis=-1)
    q = rms_norm(q.reshape(bsz, seq, ATT_HEADS, HEAD_DIM), q_gain)
    k = rms_norm(k.reshape(bsz, seq, ATT_HEADS, HEAD_DIM), k_gain)
    v = v.reshape(bsz, seq, ATT_HEADS, HEAD_DIM)
    slopes = jnp.exp2(-8.0 / ATT_HEADS * jnp.arange(1, ATT_HEADS + 1, dtype=jnp.float32))
    outs, maxes, dens = [], [], []
    for window, dilation in DILATED_PATTERNS:
        o, m, s = dilated_branch(q, k, v, slopes, window, dilation)
        outs.append(o)
        maxes.append(m)
        dens.append(s)
    o = jnp.stack(outs)
    m = jnp.stack(maxes)
    s = jnp.stack(dens)
    wgt = s * jnp.exp(m - jnp.max(m, axis=0, keepdims=True))
    y = jnp.sum(wgt[..., None] * o, axis=0) / jnp.sum(wgt, axis=0)[..., None]
    return rms_norm(y.reshape(bsz, seq, GROUP_WIDTH), out_gain)


def rwkv7_mixer(p, mu, w0, w2, a0, a2, g2, k_k, k_a, r_k, ln_w, ln_b):
    bsz, seq, _ = p.shape
    pf = p.astype(jnp.float32)
    prev = jnp.pad(pf, ((0, 0), (1, 0), (0, 0)))[:, :-1]
    pf = pf + mu * (prev - pf)
    r, k, v, wd, ad, gd = jnp.split(pf, RWKV_SPLITS, axis=-1)
    log_w = -RWKV_DECAY_SCALE * jax.nn.sigmoid(w0 + jnp.tanh(wd) @ w2)
    a = jax.nn.sigmoid(a0 + ad @ a2)
    g = jax.nn.sigmoid(gd) @ g2

    def heads(t):
        return t.reshape(bsz, seq, RWKV_HEADS, HEAD_DIM)

    kk = heads(k * k_k)
    kk = kk * lax.rsqrt(jnp.sum(kk * kk, axis=-1, keepdims=True) + 1e-12)
    k = k * (1.0 + (a - 1.0) * k_a)
    r_h, k_h, v_h, a_h, w_h = heads(r), heads(k), heads(v), heads(a), heads(jnp.exp(log_w))

    def step(state, inp):
        r_t, w_t, k_t, v_t, kk_t, a_t = inp
        s_a = jnp.einsum('bhvk,bhk->bhv', state, -kk_t)
        state = (state * w_t[:, :, None, :] + s_a[..., None] * (kk_t * a_t)[:, :, None, :]
                 + v_t[..., None] * k_t[:, :, None, :])
        return state, jnp.einsum('bhvk,bhk->bhv', state, r_t)

    state0 = jnp.zeros((bsz, RWKV_HEADS, HEAD_DIM, HEAD_DIM), jnp.float32)
    xs = tuple(jnp.swapaxes(t, 0, 1) for t in (r_h, w_h, k_h, v_h, kk, a_h))
    _, y = lax.scan(step, state0, xs)
    y = jnp.swapaxes(y, 0, 1)
    mean = jnp.mean(y, axis=-1, keepdims=True)
    var = jnp.mean(jnp.square(y - mean), axis=-1, keepdims=True)
    y = (y - mean) * lax.rsqrt(var + RWKV_GN_EPS)
    y = y.reshape(bsz, seq, GROUP_WIDTH) * ln_w + ln_b
    bonus = jnp.sum(r_h * k_h * r_k.reshape(RWKV_HEADS, HEAD_DIM), axis=-1, keepdims=True) * v_h
    return (y + bonus.reshape(bsz, seq, GROUP_WIDTH)) * g


def s5_combine(e1, e2):
    a1r, a1i, b1r, b1i = e1
    a2r, a2i, b2r, b2i = e2
    ar = a2r * a1r - a2i * a1i
    ai = a2r * a1i + a2i * a1r
    br = a2r[:, None] * b1r - a2i[:, None] * b1i + b2r
    bi = a2r[:, None] * b1i + a2i[:, None] * b1r + b2i
    return ar, ai, br, bi


def s5_mixer(u, a_re, a_im, b_re, b_im, c_re, c_im, log_dt, d_skip, glu_w, glu_b, out_gain):
    bsz, seq, _ = u.shape
    uf = u.astype(jnp.float32)
    ug = uf.reshape(bsz, seq, S5_GROUPS, S5_GROUP)
    a_re = a_re.astype(jnp.float32)
    a_im = a_im.astype(jnp.float32)
    dt = jnp.exp(log_dt.astype(jnp.float32))[:, None]
    mag = jnp.exp(dt * a_re)
    ab_re = mag * jnp.cos(dt * a_im)
    ab_im = mag * jnp.sin(dt * a_im)
    den = a_re * a_re + a_im * a_im
    f_re = ((ab_re - 1.0) * a_re + ab_im * a_im) / den
    f_im = (ab_im * a_re - (ab_re - 1.0) * a_im) / den
    bb_re = f_re[..., None] * b_re - f_im[..., None] * b_im
    bb_im = f_re[..., None] * b_im + f_im[..., None] * b_re
    bu_re = jnp.einsum('gpc,bsgc->sbgp', bb_re, ug)
    bu_im = jnp.einsum('gpc,bsgc->sbgp', bb_im, ug)
    a_seq_re = jnp.broadcast_to(ab_re, (seq,) + ab_re.shape)
    a_seq_im = jnp.broadcast_to(ab_im, (seq,) + ab_im.shape)
    _, _, h_re, h_im = lax.associative_scan(s5_combine, (a_seq_re, a_seq_im, bu_re, bu_im), axis=0)
    y = (jnp.einsum('gcp,sbgp->bsgc', c_re, h_re)
         - jnp.einsum('gcp,sbgp->bsgc', c_im, h_im))
    y = y.reshape(bsz, seq, GROUP_WIDTH) + d_skip * uf
    h = jax.nn.gelu(y)
    out = h * jax.nn.sigmoid(h @ glu_w + glu_b)
    return rms_norm(out, out_gain)


def moe_swiglu(h, router_w, router_b, w_gate, w_up, w_down):
    bsz, seq, dm = h.shape
    tokens = h.reshape(-1, dm)
    n_tok = tokens.shape[0]
    logits = tokens.astype(jnp.float32) @ router_w.astype(jnp.float32) + router_b.astype(jnp.float32)
    top_logit, top_idx = lax.top_k(logits, TOP_K)
    gates = jax.nn.softmax(top_logit, axis=-1)
    n_assign = n_tok * TOP_K
    flat_exp = top_idx.reshape(-1)
    flat_tok = jnp.repeat(jnp.arange(n_tok, dtype=jnp.int32), TOP_K)
    flat_gate = gates.reshape(-1)
    order = jnp.argsort(flat_exp)
    sorted_exp = flat_exp[order]
    counts = jnp.bincount(flat_exp, length=N_EXPERTS)
    padded = (counts + MOE_BLOCK - 1) // MOE_BLOCK * MOE_BLOCK
    pad_end = jnp.cumsum(padded)
    pad_start = pad_end - padded
    start = jnp.cumsum(counts) - counts
    dest = pad_start[sorted_exp] + (jnp.arange(n_assign) - start[sorted_exp])
    n_blocks = -(-n_assign // MOE_BLOCK) + N_EXPERTS
    cap = n_blocks * MOE_BLOCK
    slot_tok = jnp.zeros((cap,), jnp.int32).at[dest].set(flat_tok[order])
    slot_gate = jnp.zeros((cap,), jnp.float32).at[dest].set(flat_gate[order])
    block_exp = jnp.minimum(
        jnp.searchsorted(pad_end, jnp.arange(n_blocks) * MOE_BLOCK, side='right'), N_EXPERTS - 1)
    xs = tokens[slot_tok].reshape(n_blocks, MOE_BLOCK, dm)

    def expert_block(args):
        xb, e = args
        return (jax.nn.silu(xb @ w_gate[e]) * (xb @ w_up[e])) @ w_down[e]

    ys = lax.map(expert_block, (xs, block_exp)).reshape(cap, dm)
    ys = ys * slot_gate[:, None].astype(ys.dtype)
    out = jax.ops.segment_sum(ys, slot_tok, num_segments=n_tok)
    return out.reshape(bsz, seq, dm)


def setup_inputs(seed: int = 0) -> dict:
    key = jax.random.key(seed)
    ks = iter(jax.random.split(key, 64))
    f32 = jnp.float32

    def normal(shape, scale):
        return scale * jax.random.normal(next(ks), shape, f32)

    def gain(shape):
        return 1.0 + 0.02 * jax.random.normal(next(ks), shape, f32)

    def uniform(shape, lo, hi):
        return jax.random.uniform(next(ks), shape, f32, lo, hi)

    L, GW = DEPTH, GROUP_WIDTH
    x = normal((BATCH, SEQ, D_MODEL), 1.0)
    ln_mix = gain((L, D_MODEL))
    w_in = normal((L, D_MODEL, MIX_COLS), D_MODEL ** -0.5)
    w_out = normal((L, MIX_WIDTH, D_MODEL), MIX_WIDTH ** -0.5)
    ssd_conv_w = normal((L, SSD_CONV, SSD_XBC), SSD_CONV ** -0.5)
    ssd_conv_b = normal((L, SSD_XBC), 0.02)
    ssd_dt = jnp.exp(uniform((L, SSD_HEADS), math.log(1e-3), math.log(1e-1)))
    ssd_dt_bias = ssd_dt + jnp.log(-jnp.expm1(-ssd_dt))
    ssd_a_log = jnp.log(uniform((L, SSD_HEADS), 1.0, 16.0))
    ssd_d = gain((L, SSD_HEADS))
    ssd_norm = gain((L, GW))
    att_q_norm = gain((L, HEAD_DIM))
    att_k_norm = gain((L, HEAD_DIM))
    att_out_norm = gain((L, GW))
    rwkv_mu = uniform((L, RWKV_COLS), 0.0, 1.0)
    rwkv_w0 = uniform((L, GW), -4.0, 2.0)
    rwkv_w2 = normal((L, RWKV_DECAY_RANK, GW), 0.1)
    rwkv_a0 = normal((L, GW), 0.1)
    rwkv_a2 = normal((L, RWKV_A_RANK, GW), 0.5 * RWKV_A_RANK ** -0.5)
    rwkv_g2 = normal((L, RWKV_GATE_RANK, GW), RWKV_GATE_RANK ** -0.5)
    rwkv_k_k = 0.85 + normal((L, GW), 0.02)
    rwkv_k_a = gain((L, GW))
    rwkv_r_k = normal((L, GW), 0.1)
    rwkv_ln_w = gain((L, GW))
    rwkv_ln_b = normal((L, GW), 0.02)
    s5_a_re = -0.5 + normal((L, S5_GROUPS, S5_STATE), 0.01)
    s5_a_im = math.pi * jnp.arange(S5_STATE, dtype=f32) + normal((L, S5_GROUPS, S5_STATE), 0.01)
    s5_b_re = normal((L, S5_GROUPS, S5_STATE, S5_GROUP), (2 * S5_GROUP) ** -0.5)
    s5_b_im = normal((L, S5_GROUPS, S5_STATE, S5_GROUP), (2 * S5_GROUP) ** -0.5)
    s5_c_re = normal((L, S5_GROUPS, S5_GROUP, S5_STATE), S5_STATE ** -0.5)
    s5_c_im = normal((L, S5_GROUPS, S5_GROUP, S5_STATE), S5_STATE ** -0.5)
    s5_log_dt = uniform((L, S5_GROUPS), math.log(1e-3), math.log(1e-1))
    s5_d = normal((L, GW), 1.0)
    s5_glu_w = normal((L, GW, GW), GW ** -0.5)
    s5_glu_b = normal((L, GW), 0.02)
    s5_out_norm = gain((L, GW))
    ln_ffn = gain((L, D_MODEL))
    ffn_w_gate = normal((N_DENSE, D_MODEL, D_FF), D_MODEL ** -0.5)
    ffn_w_up = normal((N_DENSE, D_MODEL, D_FF), D_MODEL ** -0.5)
    ffn_w_down = normal((N_DENSE, D_FF, D_MODEL), D_FF ** -0.5)
    moe_router_w = normal((N_MOE, D_MODEL, N_EXPERTS), D_MODEL ** -0.5)
    moe_router_b = normal((N_MOE, N_EXPERTS), 0.01)
    moe_w_gate = normal((N_MOE, N_EXPERTS, D_MODEL, D_FF), D_MODEL ** -0.5)
    moe_w_up = normal((N_MOE, N_EXPERTS, D_MODEL, D_FF), D_MODEL ** -0.5)
    moe_w_down = normal((N_MOE, N_EXPERTS, D_FF, D_MODEL), D_FF ** -0.5)
    return {
        'x': x, 'ln_mix': ln_mix, 'w_in': w_in, 'w_out': w_out,
        'ssd_conv_w': ssd_conv_w, 'ssd_conv_b': ssd_conv_b, 'ssd_dt_bias': ssd_dt_bias,
        'ssd_a_log': ssd_a_log, 'ssd_d': ssd_d, 'ssd_norm': ssd_norm,
        'att_q_norm': att_q_norm, 'att_k_norm': att_k_norm, 'att_out_norm': att_out_norm,
        'rwkv_mu': rwkv_mu, 'rwkv_w0': rwkv_w0, 'rwkv_w2': rwkv_w2, 'rwkv_a0': rwkv_a0,
        'rwkv_a2': rwkv_a2, 'rwkv_g2': rwkv_g2, 'rwkv_k_k': rwkv_k_k, 'rwkv_k_a': rwkv_k_a,
        'rwkv_r_k': rwkv_r_k, 'rwkv_ln_w': rwkv_ln_w, 'rwkv_ln_b': rwkv_ln_b,
        's5_a_re': s5_a_re, 's5_a_im': s5_a_im, 's5_b_re': s5_b_re, 's5_b_im': s5_b_im,
        's5_c_re': s5_c_re, 's5_c_im': s5_c_im, 's5_log_dt': s5_log_dt, 's5_d': s5_d,
        's5_glu_w': s5_glu_w, 's5_glu_b': s5_glu_b, 's5_out_norm': s5_out_norm,
        'ln_ffn': ln_ffn, 'ffn_w_gate': ffn_w_gate, 'ffn_w_up': ffn_w_up, 'ffn_w_down': ffn_w_down,
        'moe_router_w': moe_router_w, 'moe_router_b': moe_router_b,
        'moe_w_gate': moe_w_gate, 'moe_w_up': moe_w_up, 'moe_w_down': moe_w_down,
    }


def reference(x, ln_mix, w_in, w_out,
              ssd_conv_w, ssd_conv_b, ssd_dt_bias, ssd_a_log, ssd_d, ssd_norm,
              att_q_norm, att_k_norm, att_out_norm,
              rwkv_mu, rwkv_w0, rwkv_w2, rwkv_a0, rwkv_a2, rwkv_g2, rwkv_k_k, rwkv_k_a,
              rwkv_r_k, rwkv_ln_w, rwkv_ln_b,
              s5_a_re, s5_a_im, s5_b_re, s5_b_im, s5_c_re, s5_c_im, s5_log_dt, s5_d,
              s5_glu_w, s5_glu_b, s5_out_norm,
              ln_ffn, ffn_w_gate, ffn_w_up, ffn_w_down,
              moe_router_w, moe_router_b, moe_w_gate, moe_w_up, moe_w_down):
    for layer in range(DEPTH):
        h = rms_norm(x, ln_mix[layer])
        proj = h @ w_in[layer]
        xbc, z, dt_raw, qkv, rwkv_in, s5_in = jnp.split(proj, MIX_SPLITS, axis=-1)
        y_a = ssd_mixer(xbc, z, dt_raw, ssd_conv_w[layer], ssd_conv_b[layer],
                        ssd_dt_bias[layer], ssd_a_log[layer], ssd_d[layer], ssd_norm[layer])
        y_b = dilated_attention(qkv, att_q_norm[layer], att_k_norm[layer], att_out_norm[layer])
        y_c = rwkv7_mixer(rwkv_in, rwkv_mu[layer], rwkv_w0[layer], rwkv_w2[layer], rwkv_a0[layer],
                          rwkv_a2[layer], rwkv_g2[layer], rwkv_k_k[layer], rwkv_k_a[layer],
                          rwkv_r_k[layer], rwkv_ln_w[layer], rwkv_ln_b[layer])
        y_d = s5_mixer(s5_in, s5_a_re[layer], s5_a_im[layer], s5_b_re[layer], s5_b_im[layer],
                       s5_c_re[layer], s5_c_im[layer], s5_log_dt[layer], s5_d[layer],
                       s5_glu_w[layer], s5_glu_b[layer], s5_out_norm[layer])
        mixed = jnp.concatenate([y_a, y_b, y_c, y_d], axis=-1).astype(x.dtype)
        x = x + mixed @ w_out[layer]
        h = rms_norm(x, ln_ffn[layer])
        idx = layer // 2
        if layer % 2 == 0:
            x = x + swiglu(h, ffn_w_gate[idx], ffn_w_up[idx], ffn_w_down[idx])
        else:
            x = x + moe_swiglu(h, moe_router_w[idx], moe_router_b[idx],
                               moe_w_gate[idx], moe_w_up[idx], moe_w_down[idx])
    return x
```

```python
import functools
import math

import jax
import jax.numpy as jnp
from jax import lax
from jax.experimental import pallas as pl
from jax.experimental.pallas import tpu as pltpu

F32 = jnp.float32
BF16 = jnp.bfloat16

D_MODEL = 1024
GROUP_WIDTH = 256
HEAD_DIM = 64
NORM_EPS = 1e-6
D_FF = 2816

SSD_XBC = 512
SSD_SEG = 1024
ATT_SEG = 768
RWKV_SEG = 1024
S5_SEG = 256
SEG_WIDTHS = (SSD_SEG, ATT_SEG, RWKV_SEG, S5_SEG)
SEG_TOTAL = sum(SEG_WIDTHS)

VMEM_LIMIT = 56 * 1024 * 1024


def _params(sem):
    return pltpu.CompilerParams(dimension_semantics=sem, vmem_limit_bytes=VMEM_LIMIT)


def _rms(x, gain):
    ms = jnp.mean(x * x, axis=-1, keepdims=True)
    return x * lax.rsqrt(ms + NORM_EPS) * gain


def _inproj_body(x_ref, g_ref, w_ref, o_ssd, o_att, o_rwkv, o_s5):
    h = _rms(x_ref[...], g_ref[...]).astype(BF16)
    off = 0
    for o_ref in (o_ssd, o_att, o_rwkv, o_s5):
        n = o_ref.shape[-1]
        o_ref[...] = jnp.dot(h, w_ref[:, off:off + n], preferred_element_type=F32)
        off += n


def inproj(x2, gain, w_seg, bsz, seq, tm=512):
    n_tok = x2.shape[0]
    tiles_per_seq = seq // tm
    row = lambda i: (i, 0)
    return pl.pallas_call(
        _inproj_body,
        grid=(n_tok // tm,),
        in_specs=[pl.BlockSpec((tm, D_MODEL), row),
                  pl.BlockSpec((1, D_MODEL), lambda i: (0, 0)),
                  pl.BlockSpec((D_MODEL, SEG_TOTAL), lambda i: (0, 0))],
        out_specs=[pl.BlockSpec((tm, SSD_SEG), row),
                   pl.BlockSpec((tm, ATT_SEG), row),
                   pl.BlockSpec((tm, RWKV_SEG), row),
                   pl.BlockSpec((tm, S5_SEG), lambda i: (i % tiles_per_seq, i // tiles_per_seq))],
        out_shape=[jax.ShapeDtypeStruct((n_tok, SSD_SEG), F32),
                   jax.ShapeDtypeStruct((n_tok, ATT_SEG), F32),
                   jax.ShapeDtypeStruct((n_tok, RWKV_SEG), F32),
                   jax.ShapeDtypeStruct((seq, bsz * S5_SEG), F32)],
        compiler_params=_params(("parallel",)),
        name="inproj",
    )(x2, gain.reshape(1, D_MODEL), w_seg)


def _segment_w_in(w):
    dt_cols = jnp.repeat(w[:, 768:772], HEAD_DIM, axis=1)
    return jnp.concatenate([w[:, :768], dt_cols, w[:, 772:]], axis=1).astype(BF16)


def _outproj_body(x_ref, ya_ref, yb_ref, yc_ref, yd_ref, w_ref, o_ref):
    acc = x_ref[...]
    for j, y_ref in enumerate((ya_ref, yb_ref, yc_ref, yd_ref)):
        acc = acc + jnp.dot(y_ref[...].astype(BF16), w_ref[j * GROUP_WIDTH:(j + 1) * GROUP_WIDTH, :],
                            preferred_element_type=F32)
    o_ref[...] = acc


def outproj(x2, ya, yb, yc, yd_tm, w_out_bf, bsz, seq, tm=512):
    n_tok = x2.shape[0]
    tiles_per_seq = seq // tm
    row = lambda i: (i, 0)
    return pl.pallas_call(
        _outproj_body,
        grid=(n_tok // tm,),
        in_specs=[pl.BlockSpec((tm, D_MODEL), row),
                  pl.BlockSpec((tm, GROUP_WIDTH), row),
                  pl.BlockSpec((tm, GROUP_WIDTH), row),
                  pl.BlockSpec((tm, GROUP_WIDTH), row),
                  pl.BlockSpec((tm, GROUP_WIDTH), lambda i: (i % tiles_per_seq, i // tiles_per_seq)),
                  pl.BlockSpec((D_MODEL, D_MODEL), lambda i: (0, 0))],
        out_specs=pl.BlockSpec((tm, D_MODEL), row),
        out_shape=jax.ShapeDtypeStruct((n_tok, D_MODEL), F32),
        compiler_params=_params(("parallel",)),
        name="outproj",
    )(x2, ya, yb, yc, yd_tm, w_out_bf)


def _ffn_body(x_ref, g_ref, wg_ref, wu_ref, wd_ref, o_ref, h_scr, acc_scr):
    j = pl.program_id(1)

    @pl.when(j == 0)
    def _():
        h_scr[...] = _rms(x_ref[...], g_ref[...]).astype(BF16)
        acc_scr[...] = jnp.zeros_like(acc_scr)

    h = h_scr[...]
    gate = jnp.dot(h, wg_ref[...], preferred_element_type=F32)
    up = jnp.dot(h, wu_ref[...], preferred_element_type=F32)
    act = (gate * jax.nn.sigmoid(gate) * up).astype(BF16)
    acc_scr[...] += jnp.dot(act, wd_ref[...], preferred_element_type=F32)

    @pl.when(j == pl.num_programs(1) - 1)
    def _():
        o_ref[...] = x_ref[...] + acc_scr[...]


def dense_ffn(x2, gain, wg, wu, wd, tm=512, tf=1408):
    n_tok = x2.shape[0]
    return pl.pallas_call(
        _ffn_body,
        grid=(n_tok // tm, D_FF // tf),
        in_specs=[pl.BlockSpec((tm, D_MODEL), lambda i, j: (i, 0)),
                  pl.BlockSpec((1, D_MODEL), lambda i, j: (0, 0)),
                  pl.BlockSpec((D_MODEL, tf), lambda i, j: (0, j)),
                  pl.BlockSpec((D_MODEL, tf), lambda i, j: (0, j)),
                  pl.BlockSpec((tf, D_MODEL), lambda i, j: (j, 0))],
        out_specs=pl.BlockSpec((tm, D_MODEL), lambda i, j: (i, 0)),
        out_shape=jax.ShapeDtypeStruct((n_tok, D_MODEL), F32),
        scratch_shapes=[pltpu.VMEM((tm, D_MODEL), BF16), pltpu.VMEM((tm, D_MODEL), F32)],
        compiler_params=_params(("parallel", "arbitrary")),
        name="dense_ffn",
    )(x2, gain.reshape(1, D_MODEL), wg, wu, wd)


S5_GROUPS = 16
S5_GROUP = 16
S5_STATE = 64
S5_CH = S5_GROUPS * S5_STATE
S5_BT = 8


def _s5_body(u_ref, wb_ref, wc_ref, are_ref, aim_ref, d_ref, gw_ref, gb_ref, gain_ref, o_ref, h_scr, st_scr):
    tt = u_ref.shape[0]

    @pl.when(pl.program_id(1) == 0)
    def _():
        st_scr[...] = jnp.zeros_like(st_scr)

    u2 = u_ref[...].reshape(tt * S5_BT, GROUP_WIDTH)
    bu = jnp.dot(u2.astype(BF16), wb_ref[...], preferred_element_type=F32)
    h_scr[...] = bu.reshape(tt, S5_BT, 2 * S5_CH)
    a_re = jnp.broadcast_to(are_ref[...], (S5_BT, S5_CH))
    a_im = jnp.broadcast_to(aim_ref[...], (S5_BT, S5_CH))

    def step(t, carry):
        h_re, h_im = carry
        n_re = a_re * h_re - a_im * h_im + h_scr[t, :, :S5_CH]
        n_im = a_re * h_im + a_im * h_re + h_scr[t, :, S5_CH:]
        h_scr[t, :, :S5_CH] = n_re
        h_scr[t, :, S5_CH:] = n_im
        return n_re, n_im

    h_re, h_im = lax.fori_loop(0, tt, step, (st_scr[0], st_scr[1]), unroll=4)
    st_scr[0] = h_re
    st_scr[1] = h_im

    hall = h_scr[...].reshape(tt * S5_BT, 2 * S5_CH)
    y = jnp.dot(hall.astype(BF16), wc_ref[...], preferred_element_type=F32) + d_ref[...] * u2
    hh = jax.nn.gelu(y)
    gl = jnp.dot(hh.astype(BF16), gw_ref[...], preferred_element_type=F32) + gb_ref[...]
    out = _rms(hh * jax.nn.sigmoid(gl), gain_ref[...])
    o_ref[...] = out.reshape(tt, S5_BT, GROUP_WIDTH)


def _s5_weights(a_re, a_im, b_re, b_im, c_re, c_im, log_dt):
    dt = jnp.exp(log_dt)[:, None]
    mag = jnp.exp(dt * a_re)
    ab_re = mag * jnp.cos(dt * a_im)
    ab_im = mag * jnp.sin(dt * a_im)
    den = a_re * a_re + a_im * a_im
    f_re = ((ab_re - 1.0) * a_re + ab_im * a_im) / den
    f_im = (ab_im * a_re - (ab_re - 1.0) * a_im) / den
    bb_re = f_re[..., None] * b_re - f_im[..., None] * b_im
    bb_im = f_re[..., None] * b_im + f_im[..., None] * b_re
    eye = jnp.eye(S5_GROUPS, dtype=F32)
    expand_b = lambda m: jnp.einsum('gpc,gh->gchp', m, eye).reshape(GROUP_WIDTH, S5_CH)
    expand_c = lambda m: jnp.einsum('gcp,gh->gphc', m, eye).reshape(S5_CH, GROUP_WIDTH)
    wb = jnp.concatenate([expand_b(bb_re), expand_b(bb_im)], axis=1).astype(BF16)
    wc = jnp.concatenate([expand_c(c_re), -expand_c(c_im)], axis=0).astype(BF16)
    return wb, wc, ab_re.reshape(1, S5_CH), ab_im.reshape(1, S5_CH)


def s5_mixer(u_tm, a_re, a_im, b_re, b_im, c_re, c_im, log_dt, d_skip, glu_w, glu_b, out_gain, bsz, seq, tt=128):
    wb, wc, are, aim = _s5_weights(a_re, a_im, b_re, b_im, c_re, c_im, log_dt)
    u3 = u_tm.reshape(seq, bsz, GROUP_WIDTH)
    const = lambda shape: pl.BlockSpec(shape, lambda i, j: (0,) * len(shape))
    row = lambda v: v.reshape(1, GROUP_WIDTH)
    return pl.pallas_call(
        _s5_body,
        grid=(bsz // S5_BT, seq // tt),
        in_specs=[pl.BlockSpec((tt, S5_BT, GROUP_WIDTH), lambda i, j: (j, i, 0)),
                  const((GROUP_WIDTH, 2 * S5_CH)), const((2 * S5_CH, GROUP_WIDTH)),
                  const((1, S5_CH)), const((1, S5_CH)), const((1, GROUP_WIDTH)),
                  const((GROUP_WIDTH, GROUP_WIDTH)), const((1, GROUP_WIDTH)), const((1, GROUP_WIDTH))],
        out_specs=pl.BlockSpec((tt, S5_BT, GROUP_WIDTH), lambda i, j: (j, i, 0)),
        out_shape=jax.ShapeDtypeStruct((seq, bsz, GROUP_WIDTH), F32),
        scratch_shapes=[pltpu.VMEM((tt, S5_BT, 2 * S5_CH), F32), pltpu.VMEM((2, S5_BT, S5_CH), F32)],
        compiler_params=_params(("parallel", "arbitrary")),
        name="s5_mixer",
    )(u3, wb, wc, are, aim, row(d_skip), glu_w.astype(BF16), row(glu_b), row(out_gain))


SSD_CHUNK = 128
SSD_CONV = 4
SSD_HEADS = 4
SSD_TAIL = 8
HIGHEST = lax.Precision.HIGHEST


def _dot_nt(a, b, **kw):
    return lax.dot_general(a, b, (((1,), (1,)), ((), ())), preferred_element_type=F32, **kw)


def _dot_tn(a, b, **kw):
    return lax.dot_general(a, b, (((0,), (0,)), ((), ())), preferred_element_type=F32, **kw)


def _ssd_body(p_ref, cw_ref, cb_ref, dtb_ref, arow_ref, drow_ref, gain_ref, o_ref, tail_scr, st_scr):
    tb = p_ref.shape[0]
    L = SSD_CHUNK

    @pl.when(pl.program_id(1) == 0)
    def _():
        tail_scr[...] = jnp.zeros_like(tail_scr)
        st_scr[...] = jnp.zeros_like(st_scr)

    raw = p_ref[:, :SSD_XBC]
    ext = jnp.concatenate([tail_scr[...], raw], axis=0)
    conv = cb_ref[...] + cw_ref[SSD_CONV - 1:SSD_CONV, :] * raw
    for j in range(SSD_CONV - 1):
        shifted = pltpu.roll(ext, SSD_CONV - 1 - j, axis=0)[SSD_TAIL:, :]
        conv = conv + cw_ref[j:j + 1, :] * shifted
    tail_scr[...] = raw[tb - SSD_TAIL:, :]
    xc = conv * jax.nn.sigmoid(conv)

    dt = jax.nn.softplus(p_ref[:, 768:1024] + dtb_ref[...])
    a = dt * arow_ref[...]

    ii = lax.broadcasted_iota(jnp.int32, (L, L), 0)
    jj = lax.broadcasted_iota(jnp.int32, (L, L), 1)
    causal = ii >= jj
    tril = causal.astype(F32)
    low_half = jj < HEAD_DIM
    lane2 = lax.broadcasted_iota(jnp.int32, (L, GROUP_WIDTH), 1)
    lane1 = lax.broadcasted_iota(jnp.int32, (L, L), 1)
    row_grp = lax.broadcasted_iota(jnp.int32, (L, GROUP_WIDTH), 0) // HEAD_DIM
    state_mask = row_grp == lane2 // (2 * HEAD_DIM)

    for c in range(tb // L):
        sl = slice(c * L, (c + 1) * L)
        xs = xc[sl, :GROUP_WIDTH]
        bm = xc[sl, GROUP_WIDTH:GROUP_WIDTH + L]
        cm = xc[sl, GROUP_WIDTH + L:]
        xdt = xs * dt[sl]
        cs = jnp.dot(tril, a[sl], preferred_element_type=F32, precision=HIGHEST)
        cs_end = cs[L - 1:L, :]
        st = st_scr[...]
        y = jnp.exp(cs) * jnp.dot(cm.astype(BF16), st.astype(BF16), preferred_element_type=F32)
        for g in range(2):
            cg = jnp.where((lane1 // HEAD_DIM) == g, cm, 0.0)
            gram = _dot_nt(cg.astype(BF16), bm.astype(BF16))
            pair = cs[:, g * L:(g + 1) * L]
            swapped = pltpu.roll(pair, HEAD_DIM, axis=1)
            for k in range(2):
                h = 2 * g + k
                col = jnp.where(low_half, pair, swapped) if k == 0 else jnp.where(low_half, swapped, pair)
                seg = col - col.T
                decay = jnp.where(causal, jnp.exp(jnp.minimum(seg, 0.0)), 0.0)
                yd = jnp.dot((gram * decay).astype(BF16), xdt.astype(BF16), preferred_element_type=F32)
                y = y + jnp.where((lane2 // HEAD_DIM) == h, yd, 0.0)
        upd = _dot_tn(bm.astype(BF16), (xdt * jnp.exp(cs_end - cs)).astype(BF16))
        st_scr[...] = st * jnp.exp(cs_end) + jnp.where(state_mask, upd, 0.0)
        y = y + drow_ref[...] * xs
        z = p_ref[sl, 512:768]
        o_ref[sl, :] = _rms(y * (z * jax.nn.sigmoid(z)), gain_ref[...])


def ssd_mixer(p_ssd, conv_w, conv_b, dt_bias, a_log, d_skip, norm_gain, bsz, seq, tb=512):
    n_tok = p_ssd.shape[0]
    blocks = seq // tb
    per_head = lambda v: jnp.repeat(v.astype(F32), HEAD_DIM).reshape(1, GROUP_WIDTH)
    const = lambda shape: pl.BlockSpec(shape, lambda b, j: (0,) * len(shape))
    return pl.pallas_call(
        _ssd_body,
        grid=(bsz, blocks),
        in_specs=[pl.BlockSpec((tb, SSD_SEG), lambda b, j: (b * blocks + j, 0)),
                  const((SSD_CONV, SSD_XBC)), const((1, SSD_XBC)), const((1, GROUP_WIDTH)),
                  const((1, GROUP_WIDTH)), const((1, GROUP_WIDTH)), const((1, GROUP_WIDTH))],
        out_specs=pl.BlockSpec((tb, GROUP_WIDTH), lambda b, j: (b * blocks + j, 0)),
        out_shape=jax.ShapeDtypeStruct((n_tok, GROUP_WIDTH), F32),
        scratch_shapes=[pltpu.VMEM((SSD_TAIL, SSD_XBC), F32), pltpu.VMEM((2 * HEAD_DIM, GROUP_WIDTH), F32)],
        compiler_params=_params(("parallel", "arbitrary")),
        name="ssd_mixer",
    )(p_ssd, conv_w, conv_b.reshape(1, SSD_XBC), per_head(dt_bias), per_head(-jnp.exp(a_log)), per_head(d_skip),
      norm_gain.reshape(1, GROUP_WIDTH))


ATT_HEADS = 4
ATT_BLOCK = 128
DILATED_PATTERNS = ((128, 1), (512, 4), (2048, 16))
MASKED = -1e30


def _att_bias():
    slopes = [2.0 ** (-8.0 / ATT_HEADS * (h + 1)) for h in range(ATT_HEADS)]
    q_idx = jnp.arange(ATT_BLOCK)[:, None]
    k_idx = jnp.arange(2 * ATT_BLOCK)[None, :] - ATT_BLOCK
    rel = q_idx - k_idx
    out = []
    for window, dilation in DILATED_PATTERNS:
        span = window // dilation
        valid = (rel >= 0) & (rel <= span)
        per_head = [jnp.where(valid, -s * (rel * dilation).astype(F32), MASKED) for s in slopes]
        out.append(jnp.stack(per_head))
    return jnp.stack(out)


def _ld2(scr, rows):
    return jnp.concatenate([scr[0, rows, :], scr[1, rows, :]], axis=1)


def _st2(scr, rows, val):
    scr[0, rows, :] = val[:, :ATT_BLOCK]
    scr[1, rows, :] = val[:, ATT_BLOCK:]


def _att_body(qkv_ref, gmat_ref, qg_ref, kg_ref, og_ref, bias_ref, o_ref, q_scr, k_scr, v_scr, u_scr, m_scr, s_scr):
    seq = qkv_ref.shape[0]
    blk = ATT_BLOCK
    lane = lax.broadcasted_iota(jnp.int32, (blk, GROUP_WIDTH), 1) // HEAD_DIM
    every = pl.ds(0, seq)

    def head_norm(t, gain):
        ms = jnp.dot(t * t, gmat_ref[...], preferred_element_type=F32, precision=HIGHEST)
        return t * lax.rsqrt(ms + NORM_EPS) * gain

    _st2(q_scr, every, head_norm(qkv_ref[:, :GROUP_WIDTH], qg_ref[...]) * (1.0 / math.sqrt(HEAD_DIM)))
    _st2(k_scr, every, head_norm(qkv_ref[:, GROUP_WIDTH:2 * GROUP_WIDTH], kg_ref[...]))
    _st2(v_scr, every, qkv_ref[:, 2 * GROUP_WIDTH:])

    def block(pat, rows, prev_rows, has_prev):
        q = _ld2(q_scr, rows)
        k_cur = _ld2(k_scr, rows).astype(BF16)
        v_cur = _ld2(v_scr, rows).astype(BF16)
        k_prev = _ld2(k_scr, prev_rows).astype(BF16)
        v_prev = _ld2(v_scr, prev_rows).astype(BF16)
        gate_prev = jnp.where(has_prev, 0.0, MASKED)
        u_all = jnp.zeros((blk, GROUP_WIDTH), F32)
        m_all = jnp.zeros((blk, GROUP_WIDTH), F32)
        s_all = jnp.zeros((blk, GROUP_WIDTH), F32)
        for h in range(ATT_HEADS):
            qh = jnp.where(lane == h, q, 0.0).astype(BF16)
            sc_cur = _dot_nt(qh, k_cur) + bias_ref[pat, h, :, blk:]
            sc_prev = _dot_nt(qh, k_prev) + (bias_ref[pat, h, :, :blk] + gate_prev)
            m = jnp.maximum(jnp.max(sc_cur, axis=-1, keepdims=True), jnp.max(sc_prev, axis=-1, keepdims=True))
            p_cur = jnp.exp(sc_cur - m)
            p_prev = jnp.exp(sc_prev - m)
            den = jnp.sum(p_cur, axis=-1, keepdims=True) + jnp.sum(p_prev, axis=-1, keepdims=True)
            u = (jnp.dot(p_cur.astype(BF16), v_cur, preferred_element_type=F32)
                 + jnp.dot(p_prev.astype(BF16), v_prev, preferred_element_type=F32))
            sel = lane == h
            u_all = jnp.where(sel, u, u_all)
            m_all = jnp.where(sel, m, m_all)
            s_all = jnp.where(sel, den, s_all)
        return u_all, m_all, s_all

    def merge(rows, u, m, s):
        m0 = _ld2(m_scr, rows)
        m_new = jnp.maximum(m0, m)
        alpha = jnp.exp(m0 - m_new)
        beta = jnp.exp(m - m_new)
        return alpha * _ld2(u_scr, rows) + beta * u, m_new, alpha * _ld2(s_scr, rows) + beta * s

    def save(rows, u, m, s):
        _st2(u_scr, rows, u)
        _st2(m_scr, rows, m)
        _st2(s_scr, rows, s)

    def body16(r, _):
        rows = pl.ds(r, blk, stride=16)
        save(rows, *block(2, rows, rows, False))
        return 0
    lax.fori_loop(0, 16, body16, 0)

    def body4(i, _):
        r = i % 4
        n = i // 4
        rows = pl.ds(n * (4 * blk) + r, blk, stride=4)
        prev_rows = pl.ds(jnp.maximum(n - 1, 0) * (4 * blk) + r, blk, stride=4)
        save(rows, *merge(rows, *block(1, rows, prev_rows, n > 0)))
        return 0
    lax.fori_loop(0, 16, body4, 0)

    def body1(n, _):
        start = pl.multiple_of(n * blk, blk)
        rows = pl.ds(start, blk)
        prev_rows = pl.ds(pl.multiple_of(jnp.maximum(n - 1, 0) * blk, blk), blk)
        u, m, s = block(0, rows, prev_rows, n > 0)
        u, m, s = merge(rows, u, m, s)
        o_ref[rows, :] = _rms(u / s, og_ref[...])
        return 0
    lax.fori_loop(0, seq // blk, body1, 0)


def attention_mixer(qkv, q_gain, k_gain, out_gain, bsz, seq):
    n_tok = qkv.shape[0]
    head_id = jnp.arange(GROUP_WIDTH) // HEAD_DIM
    gmat = (head_id[:, None] == head_id[None, :]).astype(F32) / HEAD_DIM
    tile = lambda g: jnp.tile(g.astype(F32), ATT_HEADS).reshape(1, GROUP_WIDTH)
    const = lambda shape: pl.BlockSpec(shape, lambda b: (0,) * len(shape))
    return pl.pallas_call(
        _att_body,
        grid=(bsz,),
        in_specs=[pl.BlockSpec((seq, ATT_SEG), lambda b: (b, 0)),
                  const((GROUP_WIDTH, GROUP_WIDTH)), const((1, GROUP_WIDTH)), const((1, GROUP_WIDTH)),
                  const((1, GROUP_WIDTH)), const((3, ATT_HEADS, ATT_BLOCK, 2 * ATT_BLOCK))],
        out_specs=pl.BlockSpec((seq, GROUP_WIDTH), lambda b: (b, 0)),
        out_shape=jax.ShapeDtypeStruct((n_tok, GROUP_WIDTH), F32),
        scratch_shapes=[pltpu.VMEM((2, seq, ATT_BLOCK), F32) for _ in range(6)],
        compiler_params=_params(("parallel",)),
        name="attention_mixer",
    )(qkv, gmat, tile(q_gain), tile(k_gain), out_gain.reshape(1, GROUP_WIDTH), _att_bias())


RWKV_HEADS = 4
RWKV_CHUNK = 64
RWKV_DECAY_SCALE = 0.606531
RWKV_GN_EPS = 64e-5
RWKV_MAT = RWKV_HEADS * RWKV_CHUNK


def _per_head_blocks(x):
    lane_head = lax.broadcasted_iota(jnp.int32, x.shape, 1) // HEAD_DIM
    return jnp.concatenate([jnp.where(lane_head == h, x, 0.0) for h in range(RWKV_HEADS)], axis=0)


def _rwkv_body(p_ref, mu_ref, w0_ref, w2_ref, a0_ref, a2_ref, g2_ref, kk_ref, ka_ref, rk_ref, lnw_ref, lnb_ref,
               hsum_ref, o_ref, prev_scr, z_scr):
    tb = p_ref.shape[0]
    L = RWKV_CHUNK
    mm = functools.partial(jnp.dot, preferred_element_type=F32, precision=HIGHEST)

    @pl.when(pl.program_id(1) == 0)
    def _():
        prev_scr[...] = jnp.zeros_like(prev_scr)
        z_scr[...] = jnp.zeros_like(z_scr)

    ii = lax.broadcasted_iota(jnp.int32, (L, RWKV_MAT), 0)
    jj = lax.broadcasted_iota(jnp.int32, (L, RWKV_MAT), 1) % L
    eye_all = (ii == jj).astype(F32)
    ti = lax.broadcasted_iota(jnp.int32, (L, L), 0)
    tj = lax.broadcasted_iota(jnp.int32, (L, L), 1)
    tril = (ti >= tj).astype(F32)
    first_row = lax.broadcasted_iota(jnp.int32, (L, RWKV_SEG), 0) == 0
    zi = lax.broadcasted_iota(jnp.int32, (RWKV_MAT, RWKV_MAT), 0) // HEAD_DIM
    zj = lax.broadcasted_iota(jnp.int32, (RWKV_MAT, RWKV_MAT), 1) // HEAD_DIM
    same_head = zi == zj

    def chunk(c, _):
        rows = pl.ds(pl.multiple_of(c * L, L), L)
        pc = p_ref[rows, :]
        prev = jnp.where(first_row, prev_scr[...], pltpu.roll(pc, 1, axis=0))
        prev_scr[...] = pc[L - 1:L, :]
        pf = pc + mu_ref[...] * (prev - pc)
        r = pf[:, :GROUP_WIDTH]
        k = pf[:, GROUP_WIDTH:2 * GROUP_WIDTH]
        v = pf[:, 2 * GROUP_WIDTH:3 * GROUP_WIDTH]
        wa = pf[:, 3 * GROUP_WIDTH:3 * GROUP_WIDTH + 128]
        gd = pf[:, 3 * GROUP_WIDTH + 128:]
        log_w = -RWKV_DECAY_SCALE * jax.nn.sigmoid(
            w0_ref[...] + jnp.dot(jnp.tanh(wa).astype(BF16), w2_ref[...], preferred_element_type=F32))
        alr = jax.nn.sigmoid(a0_ref[...] + jnp.dot(wa.astype(BF16), a2_ref[...], preferred_element_type=F32))
        g = jnp.dot(jax.nn.sigmoid(gd).astype(BF16), g2_ref[...], preferred_element_type=F32)
        kk = k * kk_ref[...]
        kk = kk * lax.rsqrt(mm(kk * kk, hsum_ref[...]) + 1e-12)
        k2 = k * (1.0 + (alr - 1.0) * ka_ref[...])

        cs = mm(tril, log_w)
        gam = jnp.exp(cs)
        gam_inv = jnp.exp(-cs)
        a_hat = -kk * jnp.exp(cs - log_w)
        b_hat = kk * alr * gam_inv
        k_hat = k2 * gam_inv
        r_hat = r * gam

        gram = _dot_nt(jnp.concatenate([a_hat, r_hat], axis=0),
                       jnp.concatenate([_per_head_blocks(b_hat), _per_head_blocks(k_hat)], axis=0), precision=HIGHEST)
        n_all = jnp.where(ii > jj, gram[:L, :RWKV_MAT], 0.0)
        m_all = jnp.where(ii > jj, gram[:L, RWKV_MAT:], 0.0)
        gb_all = jnp.where(ii >= jj, gram[L:, :RWKV_MAT], 0.0)
        gk_all = jnp.where(ii >= jj, gram[L:, RWKV_MAT:], 0.0)

        w_all = eye_all
        n_pow = n_all
        for _ in range(6):
            res = mm(jnp.concatenate([n_pow, w_all], axis=0), _per_head_blocks(n_pow))
            n_pow = res[:L]
            w_all = w_all + res[L:]

        t1 = mm(m_all, _per_head_blocks(v))
        at_vt = mm(w_all, jnp.concatenate([_per_head_blocks(a_hat), _per_head_blocks(t1)], axis=1))
        a_til = at_vt[:, :RWKV_MAT]
        v_til = at_vt[:, RWKV_MAT:]

        z = z_scr[...]
        ax_rx = _dot_nt(jnp.concatenate([a_til, r_hat], axis=0), z, precision=HIGHEST)
        u = ax_rx[:L] + v_til
        y = ax_rx[L:] + mm(jnp.concatenate([gb_all, gk_all], axis=1),
                           jnp.concatenate([_per_head_blocks(u), _per_head_blocks(v)], axis=0))
        d = _dot_tn(jnp.concatenate([u, v], axis=0), jnp.concatenate([b_hat, k_hat], axis=0), precision=HIGHEST)
        z_scr[...] = (z + jnp.where(same_head, d, 0.0)) * gam[L - 1:L, :]

        mean = mm(y, hsum_ref[...]) * (1.0 / HEAD_DIM)
        yc = y - mean
        var = mm(yc * yc, hsum_ref[...]) * (1.0 / HEAD_DIM)
        yn = yc * lax.rsqrt(var + RWKV_GN_EPS) * lnw_ref[...] + lnb_ref[...]
        bonus = mm(r * k2 * rk_ref[...], hsum_ref[...]) * v
        o_ref[rows, :] = (yn + bonus) * g
        return 0

    lax.fori_loop(0, tb // L, chunk, 0)


def rwkv_mixer(p_rwkv, mu, w0, w2, a0, a2, g2, k_k, k_a, r_k, ln_w, ln_b, bsz, seq, tb=512):
    n_tok = p_rwkv.shape[0]
    blocks = seq // tb
    row = lambda t: t.reshape(1, -1).astype(F32)
    w2p = jnp.concatenate([w2, jnp.zeros_like(a2)], axis=0).astype(BF16)
    a2p = jnp.concatenate([jnp.zeros_like(w2), a2], axis=0).astype(BF16)
    head_id = jnp.arange(GROUP_WIDTH) // HEAD_DIM
    hsum = (head_id[:, None] == head_id[None, :]).astype(F32)
    const = lambda shape: pl.BlockSpec(shape, lambda b, j: (0,) * len(shape))
    vec = const((1, GROUP_WIDTH))
    return pl.pallas_call(
        _rwkv_body,
        grid=(bsz, blocks),
        in_specs=[pl.BlockSpec((tb, RWKV_SEG), lambda b, j: (b * blocks + j, 0)),
                  const((1, RWKV_SEG)), vec, const((128, GROUP_WIDTH)), vec, const((128, GROUP_WIDTH)),
                  const((128, GROUP_WIDTH)), vec, vec, vec, vec, vec, const((GROUP_WIDTH, GROUP_WIDTH))],
        out_specs=pl.BlockSpec((tb, GROUP_WIDTH), lambda b, j: (b * blocks + j, 0)),
        out_shape=jax.ShapeDtypeStruct((n_tok, GROUP_WIDTH), F32),
        scratch_shapes=[pltpu.VMEM((1, RWKV_SEG), F32), pltpu.VMEM((RWKV_MAT, RWKV_MAT), F32)],
        compiler_params=_params(("parallel", "arbitrary")),
        name="rwkv_mixer",
    )(p_rwkv, row(mu), row(w0), w2p, row(a0), a2p, g2.astype(BF16), row(k_k), row(k_a), row(r_k), row(ln_w),
      row(ln_b), hsum)


N_EXPERTS = 8
TOP_K = 2
MOE_ROWS = 256
ROUTE_LANES = 128
FF_CHUNK = 256


def _router_body(x_ref, g_ref, w_ref, b_ref, o_ref):
    h = _rms(x_ref[...], g_ref[...])
    logits = jnp.dot(h, w_ref[...], preferred_element_type=F32, precision=HIGHEST) + b_ref[...]
    lane = lax.broadcasted_iota(jnp.int32, logits.shape, 1)
    m1 = jnp.max(logits, axis=-1, keepdims=True)
    i1 = jnp.min(jnp.where(logits == m1, lane, ROUTE_LANES), axis=-1, keepdims=True)
    rest = jnp.where(lane == i1, MASKED, logits)
    m2 = jnp.max(rest, axis=-1, keepdims=True)
    i2 = jnp.min(jnp.where(rest == m2, lane, ROUTE_LANES), axis=-1, keepdims=True)
    e2 = jnp.exp(m2 - m1)
    g1 = 1.0 / (1.0 + e2)
    out = jnp.where(lane == 0, i1.astype(F32), jnp.where(lane == 1, i2.astype(F32),
                    jnp.where(lane == 2, g1, jnp.where(lane == 3, e2 * g1, 0.0))))
    o_ref[...] = out


def moe_router(x2, gain, router_w, router_b, tm=512):
    n_tok = x2.shape[0]
    w = jnp.pad(router_w.astype(F32), ((0, 0), (0, ROUTE_LANES - N_EXPERTS)))
    b = jnp.pad(router_b.astype(F32), (0, ROUTE_LANES - N_EXPERTS), constant_values=MASKED).reshape(1, ROUTE_LANES)
    return pl.pallas_call(
        _router_body,
        grid=(n_tok // tm,),
        in_specs=[pl.BlockSpec((tm, D_MODEL), lambda i: (i, 0)),
                  pl.BlockSpec((1, D_MODEL), lambda i: (0, 0)),
                  pl.BlockSpec((D_MODEL, ROUTE_LANES), lambda i: (0, 0)),
                  pl.BlockSpec((1, ROUTE_LANES), lambda i: (0, 0))],
        out_specs=pl.BlockSpec((tm, ROUTE_LANES), lambda i: (i, 0)),
        out_shape=jax.ShapeDtypeStruct((n_tok, ROUTE_LANES), F32),
        compiler_params=_params(("parallel",)),
        name="moe_router",
    )(x2, gain.reshape(1, D_MODEL), w, b)


def _moe_dispatch(experts, n_tok):
    n_assign = n_tok * TOP_K
    n_blocks = n_assign // MOE_ROWS + N_EXPERTS
    cap = n_blocks * MOE_ROWS
    flat_exp = experts.reshape(-1)
    onehot = (flat_exp[:, None] == jnp.arange(N_EXPERTS, dtype=jnp.int32)[None, :]).astype(jnp.int32)
    csum = jnp.cumsum(onehot, axis=0)
    counts = csum[-1]
    rank = jnp.sum(onehot * csum, axis=1) - 1
    padded = (counts + MOE_ROWS - 1) // MOE_ROWS * MOE_ROWS
    pad_end = jnp.cumsum(padded)
    pad_start = pad_end - padded
    dest = pad_start[flat_exp] + rank
    slot_asg = jnp.full((cap,), -1, jnp.int32).at[dest].set(jnp.arange(n_assign, dtype=jnp.int32))
    valid = slot_asg >= 0
    slot_tok = jnp.where(valid, slot_asg // TOP_K, 0)
    spare = n_assign + jnp.cumsum((~valid).astype(jnp.int32)) - 1
    slot_dst = jnp.where(valid, (slot_asg % TOP_K) * n_tok + slot_asg // TOP_K, spare)
    block_exp = jnp.minimum(jnp.searchsorted(pad_end, jnp.arange(n_blocks, dtype=jnp.int32) * MOE_ROWS, side='right'),
                            N_EXPERTS - 1).astype(jnp.int32)
    return (slot_tok.reshape(n_blocks, 1, MOE_ROWS), slot_dst.reshape(n_blocks, 1, MOE_ROWS), block_exp)


def _expert_body(bexp_ref, tok_ref, dst_ref, x_hbm, g_ref, wg_ref, wu_ref, wd_ref, y_hbm, xbuf, obuf, sem_in, sem_out):
    del bexp_ref

    def row_in(r):
        return pltpu.make_async_copy(x_hbm.at[pl.ds(tok_ref[0, 0, r], 1)], xbuf.at[pl.ds(r, 1)], sem_in)

    def row_out(r):
        return pltpu.make_async_copy(obuf.at[pl.ds(r, 1)], y_hbm.at[pl.ds(dst_ref[0, 0, r], 1)], sem_out)

    def start_in(r, _):
        row_in(r).start()
        return 0

    def wait_in(r, _):
        row_in(r).wait()
        return 0

    lax.fori_loop(0, MOE_ROWS, start_in, 0)
    lax.fori_loop(0, MOE_ROWS, wait_in, 0)

    h = _rms(xbuf[...], g_ref[...]).astype(BF16)
    acc = jnp.zeros((MOE_ROWS, D_MODEL), F32)
    for j in range(D_FF // FF_CHUNK):
        cols = slice(j * FF_CHUNK, (j + 1) * FF_CHUNK)
        gate = jnp.dot(h, wg_ref[0, :, cols], preferred_element_type=F32)
        up = jnp.dot(h, wu_ref[0, :, cols], preferred_element_type=F32)
        act = (gate * jax.nn.sigmoid(gate) * up).astype(BF16)
        acc = acc + jnp.dot(act, wd_ref[0, cols, :], preferred_element_type=F32)
    obuf[...] = acc

    def start_out(r, _):
        row_out(r).start()
        return 0

    def wait_out(r, _):
        row_out(r).wait()
        return 0

    lax.fori_loop(0, MOE_ROWS, start_out, 0)
    lax.fori_loop(0, MOE_ROWS, wait_out, 0)


def moe_experts(x2, gain, slot_tok, slot_dst, block_exp, wg, wu, wd):
    n_blocks = block_exp.shape[0]
    cap = n_blocks * MOE_ROWS
    smem_rows = pl.BlockSpec((1, 1, MOE_ROWS), lambda i, be: (i, 0, 0), memory_space=pltpu.SMEM)
    grid_spec = pltpu.PrefetchScalarGridSpec(
        num_scalar_prefetch=1,
        grid=(n_blocks,),
        in_specs=[smem_rows, smem_rows,
                  pl.BlockSpec(memory_space=pl.ANY),
                  pl.BlockSpec((1, D_MODEL), lambda i, be: (0, 0)),
                  pl.BlockSpec((1, D_MODEL, D_FF), lambda i, be: (be[i], 0, 0)),
                  pl.BlockSpec((1, D_MODEL, D_FF), lambda i, be: (be[i], 0, 0)),
                  pl.BlockSpec((1, D_FF, D_MODEL), lambda i, be: (be[i], 0, 0))],
        out_specs=pl.BlockSpec(memory_space=pl.ANY),
        scratch_shapes=[pltpu.VMEM((MOE_ROWS, D_MODEL), F32), pltpu.VMEM((MOE_ROWS, D_MODEL), F32),
                        pltpu.SemaphoreType.DMA(()), pltpu.SemaphoreType.DMA(())],
    )
    return pl.pallas_call(
        _expert_body,
        grid_spec=grid_spec,
        out_shape=jax.ShapeDtypeStruct((cap, D_MODEL), F32),
        compiler_params=_params(("arbitrary",)),
        name="moe_experts",
    )(block_exp, slot_tok, slot_dst, x2, gain.reshape(1, D_MODEL), wg, wu, wd)


def _combine_body(x_ref, y1_ref, y2_ref, r_ref, o_ref):
    route = r_ref[...]
    o_ref[...] = x_ref[...] + route[:, 2:3] * y1_ref[...] + route[:, 3:4] * y2_ref[...]


def moe_combine(x2, y, route, tm=512):
    n_tok = x2.shape[0]
    tiles = n_tok // tm
    return pl.pallas_call(
        _combine_body,
        grid=(tiles,),
        in_specs=[pl.BlockSpec((tm, D_MODEL), lambda i: (i, 0)),
                  pl.BlockSpec((tm, D_MODEL), lambda i: (i, 0)),
                  pl.BlockSpec((tm, D_MODEL), lambda i: (tiles + i, 0)),
                  pl.BlockSpec((tm, ROUTE_LANES), lambda i: (i, 0))],
        out_specs=pl.BlockSpec((tm, D_MODEL), lambda i: (i, 0)),
        out_shape=jax.ShapeDtypeStruct((n_tok, D_MODEL), F32),
        compiler_params=_params(("parallel",)),
        name="moe_combine",
    )(x2, y, y, route)


def moe_ffn(x2, gain, router_w, router_b, wg, wu, wd):
    n_tok = x2.shape[0]
    route = moe_router(x2, gain, router_w, router_b)
    experts = route[:, :TOP_K].astype(jnp.int32)
    slot_tok, slot_dst, block_exp = _moe_dispatch(experts, n_tok)
    y = moe_experts(x2, gain, slot_tok, slot_dst, block_exp, wg, wu, wd)
    return moe_combine(x2, y, route)


def kernel(x, ln_mix, w_in, w_out, ssd_conv_w, ssd_conv_b, ssd_dt_bias, ssd_a_log, ssd_d, ssd_norm, att_q_norm, att_k_norm, att_out_norm, rwkv_mu, rwkv_w0, rwkv_w2, rwkv_a0, rwkv_a2, rwkv_g2, rwkv_k_k, rwkv_k_a, rwkv_r_k, rwkv_ln_w, rwkv_ln_b, s5_a_re, s5_a_im, s5_b_re, s5_b_im, s5_c_re, s5_c_im, s5_log_dt, s5_d, s5_glu_w, s5_glu_b, s5_out_norm, ln_ffn, ffn_w_gate, ffn_w_up, ffn_w_down, moe_router_w, moe_router_b, moe_w_gate, moe_w_up, moe_w_down):
    bsz, seq, dm = x.shape
    n_tok = bsz * seq
    depth = ln_mix.shape[0]
    x2 = x.reshape(n_tok, dm)
    for layer in range(depth):
        w_seg = _segment_w_in(w_in[layer])
        p_ssd, p_att, p_rwkv, p_s5 = inproj(x2, ln_mix[layer], w_seg, bsz, seq)
        y_a = ssd_mixer(p_ssd, ssd_conv_w[layer], ssd_conv_b[layer], ssd_dt_bias[layer], ssd_a_log[layer],
                        ssd_d[layer], ssd_norm[layer], bsz, seq)
        y_b = attention_mixer(p_att, att_q_norm[layer], att_k_norm[layer], att_out_norm[layer], bsz, seq)
        y_c = rwkv_mixer(p_rwkv, rwkv_mu[layer], rwkv_w0[layer], rwkv_w2[layer], rwkv_a0[layer], rwkv_a2[layer],
                         rwkv_g2[layer], rwkv_k_k[layer], rwkv_k_a[layer], rwkv_r_k[layer], rwkv_ln_w[layer],
                         rwkv_ln_b[layer], bsz, seq)
        y_d = s5_mixer(p_s5, s5_a_re[layer], s5_a_im[layer], s5_b_re[layer], s5_b_im[layer], s5_c_re[layer],
                       s5_c_im[layer], s5_log_dt[layer], s5_d[layer], s5_glu_w[layer], s5_glu_b[layer],
                       s5_out_norm[layer], bsz, seq)
        y_d_tm = y_d.reshape(seq, bsz * GROUP_WIDTH)
        x2 = outproj(x2, y_a, y_b, y_c, y_d_tm, w_out[layer].astype(BF16), bsz, seq)
        idx = layer // 2
        if layer % 2 == 0:
            x2 = dense_ffn(x2, ln_ffn[layer], ffn_w_gate[idx].astype(BF16), ffn_w_up[idx].astype(BF16),
                           ffn_w_down[idx].astype(BF16))
        else:
            x2 = moe_ffn(x2, ln_ffn[layer], moe_router_w[idx], moe_router_b[idx], moe_w_gate[idx].astype(BF16),
                         moe_w_up[idx].astype(BF16), moe_w_down[idx].astype(BF16))
    return x2.reshape(bsz, seq, dm)
```

```python
import functools
import math

import jax
import jax.numpy as jnp
from jax import lax
from jax.experimental import pallas as pl
from jax.experimental.pallas import tpu as pltpu

F32 = jnp.float32
BF16 = jnp.bfloat16

D_MODEL = 1024
GROUP_WIDTH = 256
HEAD_DIM = 64
NORM_EPS = 1e-6
D_FF = 2816

SSD_XBC = 512
SSD_SEG = 1024
ATT_SEG = 768
RWKV_SEG = 1024
S5_SEG = 256
SEG_WIDTHS = (SSD_SEG, ATT_SEG, RWKV_SEG, S5_SEG)
SEG_TOTAL = sum(SEG_WIDTHS)

VMEM_LIMIT = 56 * 1024 * 1024


def _params(sem):
    return pltpu.CompilerParams(dimension_semantics=sem, vmem_limit_bytes=VMEM_LIMIT)


def _rms(x, gain):
    ms = jnp.mean(x * x, axis=-1, keepdims=True)
    return x * lax.rsqrt(ms + NORM_EPS) * gain


def _inproj_body(x_ref, g_ref, w_ref, o_ssd, o_att, o_rwkv, o_s5):
    h = _rms(x_ref[...], g_ref[...]).astype(BF16)
    off = 0
    for o_ref in (o_ssd, o_att, o_rwkv, o_s5):
        n = o_ref.shape[-1]
        o_ref[...] = jnp.dot(h, w_ref[:, off:off + n], preferred_element_type=F32)
        off += n


def inproj(x2, gain, w_seg, bsz, seq, tm=512):
    n_tok = x2.shape[0]
    tiles_per_seq = seq // tm
    row = lambda i: (i, 0)
    return pl.pallas_call(
        _inproj_body,
        grid=(n_tok // tm,),
        in_specs=[pl.BlockSpec((tm, D_MODEL), row),
                  pl.BlockSpec((1, D_MODEL), lambda i: (0, 0)),
                  pl.BlockSpec((D_MODEL, SEG_TOTAL), lambda i: (0, 0))],
        out_specs=[pl.BlockSpec((tm, SSD_SEG), row),
                   pl.BlockSpec((tm, ATT_SEG), row),
                   pl.BlockSpec((tm, RWKV_SEG), row),
                   pl.BlockSpec((tm, S5_SEG), lambda i: (i % tiles_per_seq, i // tiles_per_seq))],
        out_shape=[jax.ShapeDtypeStruct((n_tok, SSD_SEG), F32),
                   jax.ShapeDtypeStruct((n_tok, ATT_SEG), F32),
                   jax.ShapeDtypeStruct((n_tok, RWKV_SEG), F32),
                   jax.ShapeDtypeStruct((seq, bsz * S5_SEG), F32)],
        compiler_params=_params(("parallel",)),
        name="inproj",
    )(x2, gain.reshape(1, D_MODEL), w_seg)


def _segment_w_in(w):
    dt_cols = jnp.repeat(w[:, 768:772], HEAD_DIM, axis=1)
    return jnp.concatenate([w[:, :768], dt_cols, w[:, 772:]], axis=1).astype(BF16)


def _outproj_body(x_ref, ya_ref, yb_ref, yc_ref, yd_ref, w_ref, o_ref):
    acc = x_ref[...]
    for j, y_ref in enumerate((ya_ref, yb_ref, yc_ref, yd_ref)):
        acc = acc + jnp.dot(y_ref[...].astype(BF16), w_ref[j * GROUP_WIDTH:(j + 1) * GROUP_WIDTH, :],
                            preferred_element_type=F32)
    o_ref[...] = acc


def outproj(x2, ya, yb, yc, yd_tm, w_out_bf, bsz, seq, tm=512):
    n_tok = x2.shape[0]
    tiles_per_seq = seq // tm
    row = lambda i: (i, 0)
    return pl.pallas_call(
        _outproj_body,
        grid=(n_tok // tm,),
        in_specs=[pl.BlockSpec((tm, D_MODEL), row),
                  pl.BlockSpec((tm, GROUP_WIDTH), row),
                  pl.BlockSpec((tm, GROUP_WIDTH), row),
                  pl.BlockSpec((tm, GROUP_WIDTH), row),
                  pl.BlockSpec((tm, GROUP_WIDTH), lambda i: (i % tiles_per_seq, i // tiles_per_seq)),
                  pl.BlockSpec((D_MODEL, D_MODEL), lambda i: (0, 0))],
        out_specs=pl.BlockSpec((tm, D_MODEL), row),
        out_shape=jax.ShapeDtypeStruct((n_tok, D_MODEL), F32),
        compiler_params=_params(("parallel",)),
        name="outproj",
    )(x2, ya, yb, yc, yd_tm, w_out_bf)


def _ffn_body(x_ref, g_ref, wg_ref, wu_ref, wd_ref, o_ref, h_scr, acc_scr):
    j = pl.program_id(1)

    @pl.when(j == 0)
    def _():
        h_scr[...] = _rms(x_ref[...], g_ref[...]).astype(BF16)
        acc_scr[...] = jnp.zeros_like(acc_scr)

    h = h_scr[...]
    gate = jnp.dot(h, wg_ref[...], preferred_element_type=F32)
    up = jnp.dot(h, wu_ref[...], preferred_element_type=F32)
    act = (gate * jax.nn.sigmoid(gate) * up).astype(BF16)
    acc_scr[...] += jnp.dot(act, wd_ref[...], preferred_element_type=F32)

    @pl.when(j == pl.num_programs(1) - 1)
    def _():
        o_ref[...] = x_ref[...] + acc_scr[...]


def dense_ffn(x2, gain, wg, wu, wd, tm=512, tf=1408):
    n_tok = x2.shape[0]
    return pl.pallas_call(
        _ffn_body,
        grid=(n_tok // tm, D_FF // tf),
        in_specs=[pl.BlockSpec((tm, D_MODEL), lambda i, j: (i, 0)),
                  pl.BlockSpec((1, D_MODEL), lambda i, j: (0, 0)),
                  pl.BlockSpec((D_MODEL, tf), lambda i, j: (0, j)),
                  pl.BlockSpec((D_MODEL, tf), lambda i, j: (0, j)),
                  pl.BlockSpec((tf, D_MODEL), lambda i, j: (j, 0))],
        out_specs=pl.BlockSpec((tm, D_MODEL), lambda i, j: (i, 0)),
        out_shape=jax.ShapeDtypeStruct((n_tok, D_MODEL), F32),
        scratch_shapes=[pltpu.VMEM((tm, D_MODEL), BF16), pltpu.VMEM((tm, D_MODEL), F32)],
        compiler_params=_params(("parallel", "arbitrary")),
        name="dense_ffn",
    )(x2, gain.reshape(1, D_MODEL), wg, wu, wd)


S5_GROUPS = 16
S5_GROUP = 16
S5_STATE = 64
S5_CH = S5_GROUPS * S5_STATE
S5_BT = 8


def _s5_body(u_ref, wb_ref, wc_ref, are_ref, aim_ref, d_ref, gw_ref, gb_ref, gain_ref, o_ref, h_scr, st_scr):
    tt = u_ref.shape[0]

    @pl.when(pl.program_id(1) == 0)
    def _():
        st_scr[...] = jnp.zeros_like(st_scr)

    u2 = u_ref[...].reshape(tt * S5_BT, GROUP_WIDTH)
    bu = jnp.dot(u2.astype(BF16), wb_ref[...], preferred_element_type=F32)
    h_scr[...] = bu.reshape(tt, S5_BT, 2 * S5_CH)
    a_re = jnp.broadcast_to(are_ref[...], (S5_BT, S5_CH))
    a_im = jnp.broadcast_to(aim_ref[...], (S5_BT, S5_CH))

    def step(t, carry):
        h_re, h_im = carry
        n_re = a_re * h_re - a_im * h_im + h_scr[t, :, :S5_CH]
        n_im = a_re * h_im + a_im * h_re + h_scr[t, :, S5_CH:]
        h_scr[t, :, :S5_CH] = n_re
        h_scr[t, :, S5_CH:] = n_im
        return n_re, n_im

    h_re, h_im = lax.fori_loop(0, tt, step, (st_scr[0], st_scr[1]), unroll=4)
    st_scr[0] = h_re
    st_scr[1] = h_im

    hall = h_scr[...].reshape(tt * S5_BT, 2 * S5_CH)
    y = jnp.dot(hall.astype(BF16), wc_ref[...], preferred_element_type=F32) + d_ref[...] * u2
    hh = jax.nn.gelu(y)
    gl = jnp.dot(hh.astype(BF16), gw_ref[...], preferred_element_type=F32) + gb_ref[...]
    out = _rms(hh * jax.nn.sigmoid(gl), gain_ref[...])
    o_ref[...] = out.reshape(tt, S5_BT, GROUP_WIDTH)


def _s5_weights(a_re, a_im, b_re, b_im, c_re, c_im, log_dt):
    dt = jnp.exp(log_dt)[:, None]
    mag = jnp.exp(dt * a_re)
    ab_re = mag * jnp.cos(dt * a_im)
    ab_im = mag * jnp.sin(dt * a_im)
    den = a_re * a_re + a_im * a_im
    f_re = ((ab_re - 1.0) * a_re + ab_im * a_im) / den
    f_im = (ab_im * a_re - (ab_re - 1.0) * a_im) / den
    bb_re = f_re[..., None] * b_re - f_im[..., None] * b_im
    bb_im = f_re[..., None] * b_im + f_im[..., None] * b_re
    eye = jnp.eye(S5_GROUPS, dtype=F32)
    expand_b = lambda m: jnp.einsum('gpc,gh->gchp', m, eye).reshape(GROUP_WIDTH, S5_CH)
    expand_c = lambda m: jnp.einsum('gcp,gh->gphc', m, eye).reshape(S5_CH, GROUP_WIDTH)
    wb = jnp.concatenate([expand_b(bb_re), expand_b(bb_im)], axis=1).astype(BF16)
    wc = jnp.concatenate([expand_c(c_re), -expand_c(c_im)], axis=0).astype(BF16)
    return wb, wc, ab_re.reshape(1, S5_CH), ab_im.reshape(1, S5_CH)


def s5_mixer(u_tm, a_re, a_im, b_re, b_im, c_re, c_im, log_dt, d_skip, glu_w, glu_b, out_gain, bsz, seq, tt=128):
    wb, wc, are, aim = _s5_weights(a_re, a_im, b_re, b_im, c_re, c_im, log_dt)
    u3 = u_tm.reshape(seq, bsz, GROUP_WIDTH)
    const = lambda shape: pl.BlockSpec(shape, lambda i, j: (0,) * len(shape))
    row = lambda v: v.reshape(1, GROUP_WIDTH)
    return pl.pallas_call(
        _s5_body,
        grid=(bsz // S5_BT, seq // tt),
        in_specs=[pl.BlockSpec((tt, S5_BT, GROUP_WIDTH), lambda i, j: (j, i, 0)),
                  const((GROUP_WIDTH, 2 * S5_CH)), const((2 * S5_CH, GROUP_WIDTH)),
                  const((1, S5_CH)), const((1, S5_CH)), const((1, GROUP_WIDTH)),
                  const((GROUP_WIDTH, GROUP_WIDTH)), const((1, GROUP_WIDTH)), const((1, GROUP_WIDTH))],
        out_specs=pl.BlockSpec((tt, S5_BT, GROUP_WIDTH), lambda i, j: (j, i, 0)),
        out_shape=jax.ShapeDtypeStruct((seq, bsz, GROUP_WIDTH), F32),
        scratch_shapes=[pltpu.VMEM((tt, S5_BT, 2 * S5_CH), F32), pltpu.VMEM((2, S5_BT, S5_CH), F32)],
        compiler_params=_params(("parallel", "arbitrary")),
        name="s5_mixer",
    )(u3, wb, wc, are, aim, row(d_skip), glu_w.astype(BF16), row(glu_b), row(out_gain))


SSD_CHUNK = 128
SSD_CONV = 4
SSD_HEADS = 4
SSD_TAIL = 8
HIGHEST = lax.Precision.HIGHEST


def _dot_nt(a, b, **kw):
    return lax.dot_general(a, b, (((1,), (1,)), ((), ())), preferred_element_type=F32, **kw)


def _dot_tn(a, b, **kw):
    return lax.dot_general(a, b, (((0,), (0,)), ((), ())), preferred_element_type=F32, **kw)


def _ssd_body(p_ref, cw_ref, cb_ref, dtb_ref, arow_ref, drow_ref, gain_ref, o_ref, tail_scr, st_scr):
    tb = p_ref.shape[0]
    L = SSD_CHUNK

    @pl.when(pl.program_id(1) == 0)
    def _():
        tail_scr[...] = jnp.zeros_like(tail_scr)
        st_scr[...] = jnp.zeros_like(st_scr)

    raw = p_ref[:, :SSD_XBC]
    ext = jnp.concatenate([tail_scr[...], raw], axis=0)
    conv = cb_ref[...] + cw_ref[SSD_CONV - 1:SSD_CONV, :] * raw
    for j in range(SSD_CONV - 1):
        shifted = pltpu.roll(ext, SSD_CONV - 1 - j, axis=0)[SSD_TAIL:, :]
        conv = conv + cw_ref[j:j + 1, :] * shifted
    tail_scr[...] = raw[tb - SSD_TAIL:, :]
    xc = conv * jax.nn.sigmoid(conv)

    dt = jax.nn.softplus(p_ref[:, 768:1024] + dtb_ref[...])
    a = dt * arow_ref[...]

    ii = lax.broadcasted_iota(jnp.int32, (L, L), 0)
    jj = lax.broadcasted_iota(jnp.int32, (L, L), 1)
    causal = ii >= jj
    tril = causal.astype(F32)
    low_half = jj < HEAD_DIM
    lane2 = lax.broadcasted_iota(jnp.int32, (L, GROUP_WIDTH), 1)
    lane1 = lax.broadcasted_iota(jnp.int32, (L, L), 1)
    row_grp = lax.broadcasted_iota(jnp.int32, (L, GROUP_WIDTH), 0) // HEAD_DIM
    state_mask = row_grp == lane2 // (2 * HEAD_DIM)

    for c in range(tb // L):
        sl = slice(c * L, (c + 1) * L)
        xs = xc[sl, :GROUP_WIDTH]
        bm = xc[sl, GROUP_WIDTH:GROUP_WIDTH + L]
        cm = xc[sl, GROUP_WIDTH + L:]
        xdt = xs * dt[sl]
        cs = jnp.dot(tril, a[sl], preferred_element_type=F32, precision=HIGHEST)
        cs_end = cs[L - 1:L, :]
        st = st_scr[...]
        y = jnp.exp(cs) * jnp.dot(cm.astype(BF16), st.astype(BF16), preferred_element_type=F32)
        for g in range(2):
            cg = jnp.where((lane1 // HEAD_DIM) == g, cm, 0.0)
            gram = _dot_nt(cg.astype(BF16), bm.astype(BF16))
            pair = cs[:, g * L:(g + 1) * L]
            swapped = pltpu.roll(pair, HEAD_DIM, axis=1)
            for k in range(2):
                h = 2 * g + k
                col = jnp.where(low_half, pair, swapped) if k == 0 else jnp.where(low_half, swapped, pair)
                seg = col - col.T
                decay = jnp.where(causal, jnp.exp(jnp.minimum(seg, 0.0)), 0.0)
                yd = jnp.dot((gram * decay).astype(BF16), xdt.astype(BF16), preferred_element_type=F32)
                y = y + jnp.where((lane2 // HEAD_DIM) == h, yd, 0.0)
        upd = _dot_tn(bm.astype(BF16), (xdt * jnp.exp(cs_end - cs)).astype(BF16))
        st_scr[...] = st * jnp.exp(cs_end) + jnp.where(state_mask, upd, 0.0)
        y = y + drow_ref[...] * xs
        z = p_ref[sl, 512:768]
        o_ref[sl, :] = _rms(y * (z * jax.nn.sigmoid(z)), gain_ref[...])


def ssd_mixer(p_ssd, conv_w, conv_b, dt_bias, a_log, d_skip, norm_gain, bsz, seq, tb=512):
    n_tok = p_ssd.shape[0]
    blocks = seq // tb
    per_head = lambda v: jnp.repeat(v.astype(F32), HEAD_DIM).reshape(1, GROUP_WIDTH)
    const = lambda shape: pl.BlockSpec(shape, lambda b, j: (0,) * len(shape))
    return pl.pallas_call(
        _ssd_body,
        grid=(bsz, blocks),
        in_specs=[pl.BlockSpec((tb, SSD_SEG), lambda b, j: (b * blocks + j, 0)),
                  const((SSD_CONV, SSD_XBC)), const((1, SSD_XBC)), const((1, GROUP_WIDTH)),
                  const((1, GROUP_WIDTH)), const((1, GROUP_WIDTH)), const((1, GROUP_WIDTH))],
        out_specs=pl.BlockSpec((tb, GROUP_WIDTH), lambda b, j: (b * blocks + j, 0)),
        out_shape=jax.ShapeDtypeStruct((n_tok, GROUP_WIDTH), F32),
        scratch_shapes=[pltpu.VMEM((SSD_TAIL, SSD_XBC), F32), pltpu.VMEM((2 * HEAD_DIM, GROUP_WIDTH), F32)],
        compiler_params=_params(("parallel", "arbitrary")),
        name="ssd_mixer",
    )(p_ssd, conv_w, conv_b.reshape(1, SSD_XBC), per_head(dt_bias), per_head(-jnp.exp(a_log)), per_head(d_skip),
      norm_gain.reshape(1, GROUP_WIDTH))


ATT_HEADS = 4
ATT_BLOCK = 128
DILATED_PATTERNS = ((128, 1), (512, 4), (2048, 16))
MASKED = -1e30
ATT_PAR = 4


def _att_bias():
    slopes = [2.0 ** (-8.0 / ATT_HEADS * (h + 1)) for h in range(ATT_HEADS)]
    q_idx = jnp.arange(ATT_BLOCK)[:, None]
    k_idx = jnp.arange(2 * ATT_BLOCK)[None, :] - ATT_BLOCK
    rel = q_idx - k_idx
    out = []
    for window, dilation in DILATED_PATTERNS:
        span = window // dilation
        valid = (rel >= 0) & (rel <= span)
        per_head = [jnp.where(valid, -s * (rel * dilation).astype(F32), MASKED) for s in slopes]
        out.append(jnp.stack(per_head))
    return jnp.stack(out)


def _ld2(scr, rows):
    return jnp.concatenate([scr[0, rows, :], scr[1, rows, :]], axis=1)


def _st2(scr, rows, val):
    scr[0, rows, :] = val[:, :ATT_BLOCK]
    scr[1, rows, :] = val[:, ATT_BLOCK:]


def _att_body(qkv_ref, gmat_ref, qg_ref, kg_ref, og_ref, bias_ref, o_ref, q_scr, k_scr, v_scr, u_scr, m_scr, s_scr):
    seq = qkv_ref.shape[0]
    blk = ATT_BLOCK
    lane = lax.broadcasted_iota(jnp.int32, (blk, GROUP_WIDTH), 1) // HEAD_DIM
    every = pl.ds(0, seq)

    def head_norm(t, gain):
        sq = t * t
        hi = sq.astype(BF16)
        lo = (sq - hi.astype(F32)).astype(BF16)
        ms = (jnp.dot(hi, gmat_ref[...], preferred_element_type=F32)
              + jnp.dot(lo, gmat_ref[...], preferred_element_type=F32))
        return t * lax.rsqrt(ms + NORM_EPS) * gain

    _st2(q_scr, every, head_norm(qkv_ref[:, :GROUP_WIDTH], qg_ref[...]) * (1.0 / math.sqrt(HEAD_DIM)))
    _st2(k_scr, every, head_norm(qkv_ref[:, GROUP_WIDTH:2 * GROUP_WIDTH], kg_ref[...]))
    _st2(v_scr, every, qkv_ref[:, 2 * GROUP_WIDTH:])

    prev_cols = lax.broadcasted_iota(jnp.int32, (1, 2 * blk), 1) < blk
    heads = range(ATT_HEADS)

    def blocks(pat, specs):
        pairs = [(b, h) for b in range(len(specs)) for h in heads]
        q, keys, vals, gate = [], [], [], []
        for rows, prev_rows, has_prev in specs:
            q.append(_ld2(q_scr, rows))
            if prev_rows is None:
                keys.append(_ld2(k_scr, rows).astype(BF16))
                vals.append(_ld2(v_scr, rows).astype(BF16))
                gate.append(None)
            else:
                keys.append(jnp.concatenate([_ld2(k_scr, prev_rows), _ld2(k_scr, rows)], axis=0).astype(BF16))
                vals.append(jnp.concatenate([_ld2(v_scr, prev_rows), _ld2(v_scr, rows)], axis=0).astype(BF16))
                gate.append(jnp.where(jnp.logical_or(has_prev, jnp.logical_not(prev_cols)), 0.0, MASKED))
        bias = {(b, h): (bias_ref[pat, h, :, blk:] if gate[b] is None else bias_ref[pat, h] + gate[b])
                for b, h in pairs}
        qh = {(b, h): jnp.where(lane == h, q[b], 0.0).astype(BF16) for b, h in pairs}
        sc = {(b, h): _dot_nt(qh[b, h], keys[b]) + bias[b, h] for b, h in pairs}
        m = {bh: jnp.max(sc[bh], axis=-1, keepdims=True) for bh in pairs}
        p = {bh: jnp.exp(sc[bh] - m[bh]) for bh in pairs}
        den = {bh: jnp.sum(p[bh], axis=-1, keepdims=True) for bh in pairs}
        u = {(b, h): jnp.dot(p[b, h].astype(BF16), vals[b], preferred_element_type=F32) for b, h in pairs}
        out = []
        for b in range(len(specs)):
            u_all, m_all, s_all = u[b, 0], m[b, 0], den[b, 0]
            for h in heads[1:]:
                sel = lane == h
                u_all = jnp.where(sel, u[b, h], u_all)
                m_all = jnp.where(sel, m[b, h], m_all)
                s_all = jnp.where(sel, den[b, h], s_all)
            out.append((u_all, m_all, s_all))
        return out

    def merge(rows, u, m, s):
        m0 = _ld2(m_scr, rows)
        m_new = jnp.maximum(m0, m)
        alpha = jnp.exp(m0 - m_new)
        beta = jnp.exp(m - m_new)
        return alpha * _ld2(u_scr, rows) + beta * u, m_new, alpha * _ld2(s_scr, rows) + beta * s

    def save(rows, u, m, s):
        _st2(u_scr, rows, u)
        _st2(m_scr, rows, m)
        _st2(s_scr, rows, s)

    par = ATT_PAR
    n_iter = 16 // par

    def body16(i, _):
        specs = [(pl.ds(i + n_iter * b, blk, stride=16), None, None) for b in range(par)]
        for spec, res in zip(specs, blocks(2, specs)):
            save(spec[0], *res)
        return 0
    lax.fori_loop(0, n_iter, body16, 0)

    def body4(i, _):
        specs = []
        for b in range(par):
            idx = i + n_iter * b
            r = idx % 4
            n = idx // 4
            specs.append((pl.ds(n * (4 * blk) + r, blk, stride=4),
                          pl.ds(jnp.maximum(n - 1, 0) * (4 * blk) + r, blk, stride=4), n > 0))
        for spec, res in zip(specs, blocks(1, specs)):
            save(spec[0], *merge(spec[0], *res))
        return 0
    lax.fori_loop(0, n_iter, body4, 0)

    def body1(i, _):
        specs = []
        for b in range(par):
            n = i + (seq // blk // par) * b
            specs.append((pl.ds(pl.multiple_of(n * blk, blk), blk),
                          pl.ds(pl.multiple_of(jnp.maximum(n - 1, 0) * blk, blk), blk), n > 0))
        for spec, res in zip(specs, blocks(0, specs)):
            u, m, s = merge(spec[0], *res)
            o_ref[spec[0], :] = _rms(u / s, og_ref[...])
        return 0
    lax.fori_loop(0, seq // blk // par, body1, 0)


def attention_mixer(qkv, q_gain, k_gain, out_gain, bsz, seq):
    n_tok = qkv.shape[0]
    head_id = jnp.arange(GROUP_WIDTH) // HEAD_DIM
    gmat = ((head_id[:, None] == head_id[None, :]).astype(F32) / HEAD_DIM).astype(BF16)
    tile = lambda g: jnp.tile(g.astype(F32), ATT_HEADS).reshape(1, GROUP_WIDTH)
    const = lambda shape: pl.BlockSpec(shape, lambda b: (0,) * len(shape))
    return pl.pallas_call(
        _att_body,
        grid=(bsz,),
        in_specs=[pl.BlockSpec((seq, ATT_SEG), lambda b: (b, 0)),
                  const((GROUP_WIDTH, GROUP_WIDTH)), const((1, GROUP_WIDTH)), const((1, GROUP_WIDTH)),
                  const((1, GROUP_WIDTH)), const((3, ATT_HEADS, ATT_BLOCK, 2 * ATT_BLOCK))],
        out_specs=pl.BlockSpec((seq, GROUP_WIDTH), lambda b: (b, 0)),
        out_shape=jax.ShapeDtypeStruct((n_tok, GROUP_WIDTH), F32),
        scratch_shapes=[pltpu.VMEM((2, seq, ATT_BLOCK), F32) for _ in range(6)],
        compiler_params=_params(("parallel",)),
        name="attention_mixer",
    )(qkv, gmat, tile(q_gain), tile(k_gain), out_gain.reshape(1, GROUP_WIDTH), _att_bias())


RWKV_HEADS = 4
RWKV_CHUNK = 64
RWKV_DECAY_SCALE = 0.606531
RWKV_GN_EPS = 64e-5
RWKV_MAT = RWKV_HEADS * RWKV_CHUNK


def _per_head_blocks(x):
    lane_head = lax.broadcasted_iota(jnp.int32, x.shape, 1) // HEAD_DIM
    return jnp.concatenate([jnp.where(lane_head == h, x, 0.0) for h in range(RWKV_HEADS)], axis=0)


def _rwkv_body(p_ref, mu_ref, w0_ref, w2_ref, a0_ref, a2_ref, g2_ref, kk_ref, ka_ref, rk_ref, lnw_ref, lnb_ref,
               hsum_ref, o_ref, prev_scr, z_scr):
    nb, tb = p_ref.shape[0], p_ref.shape[1]
    L = RWKV_CHUNK

    def mb(a, b):
        return jnp.dot(a.astype(BF16), b.astype(BF16), preferred_element_type=F32)

    def split(x):
        hi = x.astype(BF16)
        return hi, (x - hi.astype(F32)).astype(BF16)

    def head_sum(x):
        hi, lo = split(x)
        return (jnp.dot(hi, hsum_ref[...], preferred_element_type=F32)
                + jnp.dot(lo, hsum_ref[...], preferred_element_type=F32))

    @pl.when(pl.program_id(1) == 0)
    def _():
        prev_scr[...] = jnp.zeros_like(prev_scr)
        z_scr[...] = jnp.zeros_like(z_scr)

    ii = lax.broadcasted_iota(jnp.int32, (L, RWKV_MAT), 0)
    jj = lax.broadcasted_iota(jnp.int32, (L, RWKV_MAT), 1) % L
    eye_all = (ii == jj).astype(F32)
    ti = lax.broadcasted_iota(jnp.int32, (L, L), 0)
    tj = lax.broadcasted_iota(jnp.int32, (L, L), 1)
    tril = (ti >= tj).astype(BF16)
    first_row = lax.broadcasted_iota(jnp.int32, (L, RWKV_SEG), 0) == 0
    zi = lax.broadcasted_iota(jnp.int32, (RWKV_MAT, RWKV_MAT), 0) // HEAD_DIM
    zj = lax.broadcasted_iota(jnp.int32, (RWKV_MAT, RWKV_MAT), 1) // HEAD_DIM
    same_head = zi == zj

    def each(fn, *lists):
        return [fn(*args) for args in zip(*lists)]

    def chunks(c, _):
        seqs = list(range(nb))
        rows = pl.ds(pl.multiple_of(c * L, L), L)
        pc = [p_ref[s, rows, :] for s in seqs]
        prev = [jnp.where(first_row, prev_scr[s], pltpu.roll(pc[s], 1, axis=0)) for s in seqs]
        for s in seqs:
            prev_scr[s] = pc[s][L - 1:L, :]
        pf = each(lambda x, xp: x + mu_ref[...] * (xp - x), pc, prev)
        r = [x[:, :GROUP_WIDTH] for x in pf]
        k = [x[:, GROUP_WIDTH:2 * GROUP_WIDTH] for x in pf]
        v = [x[:, 2 * GROUP_WIDTH:3 * GROUP_WIDTH] for x in pf]
        wa = [x[:, 3 * GROUP_WIDTH:3 * GROUP_WIDTH + 128] for x in pf]
        gd = [x[:, 3 * GROUP_WIDTH + 128:] for x in pf]
        log_w = each(lambda x: -RWKV_DECAY_SCALE * jax.nn.sigmoid(
            w0_ref[...] + jnp.dot(jnp.tanh(x).astype(BF16), w2_ref[...], preferred_element_type=F32)), wa)
        alr = each(lambda x: jax.nn.sigmoid(
            a0_ref[...] + jnp.dot(x.astype(BF16), a2_ref[...], preferred_element_type=F32)), wa)
        g = each(lambda x: jnp.dot(jax.nn.sigmoid(x).astype(BF16), g2_ref[...], preferred_element_type=F32), gd)
        kk = each(lambda x: x * kk_ref[...], k)
        kk_ss = each(lambda x: head_sum(x * x), kk)
        kk = each(lambda x, ss: x * lax.rsqrt(ss + 1e-12), kk, kk_ss)
        k2 = each(lambda x, a: x * (1.0 + (a - 1.0) * ka_ref[...]), k, alr)

        def cumsum(lw):
            hi, lo = split(lw)
            return (jnp.dot(tril, hi, preferred_element_type=F32) + jnp.dot(tril, lo, preferred_element_type=F32))
        cs = each(cumsum, log_w)
        gam = each(jnp.exp, cs)
        gam_inv = each(lambda x: jnp.exp(-x), cs)
        a_hat = each(lambda x, c_, lw: -x * jnp.exp(c_ - lw), kk, cs, log_w)
        b_hat = each(lambda x, a, gi: x * a * gi, kk, alr, gam_inv)
        k_hat = each(lambda x, gi: x * gi, k2, gam_inv)
        r_hat = each(lambda x, gm: x * gm, r, gam)

        gram = each(lambda a, rr, b, kh: _dot_nt(
            jnp.concatenate([a, rr], axis=0).astype(BF16),
            jnp.concatenate([_per_head_blocks(b), _per_head_blocks(kh)], axis=0).astype(BF16)),
            a_hat, r_hat, b_hat, k_hat)
        n_all = [jnp.where(ii > jj, x[:L, :RWKV_MAT], 0.0) for x in gram]
        m_all = [jnp.where(ii > jj, x[:L, RWKV_MAT:], 0.0) for x in gram]
        gb_all = [jnp.where(ii >= jj, x[L:, :RWKV_MAT], 0.0) for x in gram]
        gk_all = [jnp.where(ii >= jj, x[L:, RWKV_MAT:], 0.0) for x in gram]

        w_all = [eye_all for _ in seqs]
        n_pow = n_all
        for _ in range(6):
            res = each(lambda n, w: mb(jnp.concatenate([n, w], axis=0), _per_head_blocks(n)), n_pow, w_all)
            n_pow = [x[:L] for x in res]
            w_all = each(lambda w, x: w + x[L:], w_all, res)

        t1 = each(lambda m, vv: mb(m, _per_head_blocks(vv)), m_all, v)
        at_vt = each(lambda w, a, t: mb(w, jnp.concatenate([_per_head_blocks(a), _per_head_blocks(t)], axis=1)),
                     w_all, a_hat, t1)
        a_til = [x[:, :RWKV_MAT] for x in at_vt]
        v_til = [x[:, RWKV_MAT:] for x in at_vt]

        z = [z_scr[s] for s in seqs]
        ax_rx = each(lambda a, rr, zz: _dot_nt(jnp.concatenate([a, rr], axis=0).astype(BF16), zz.astype(BF16)),
                     a_til, r_hat, z)
        u = each(lambda x, vt: x[:L] + vt, ax_rx, v_til)
        y = each(lambda x, gb, gk, uu, vv: x[L:] + mb(
            jnp.concatenate([gb, gk], axis=1),
            jnp.concatenate([_per_head_blocks(uu), _per_head_blocks(vv)], axis=0)), ax_rx, gb_all, gk_all, u, v)
        d = each(lambda uu, vv, b, kh: _dot_tn(jnp.concatenate([uu, vv], axis=0).astype(BF16),
                                               jnp.concatenate([b, kh], axis=0).astype(BF16)), u, v, b_hat, k_hat)
        for s in seqs:
            z_scr[s] = (z[s] + jnp.where(same_head, d[s], 0.0)) * gam[s][L - 1:L, :]

        mean = each(lambda x: head_sum(x) * (1.0 / HEAD_DIM), y)
        yc = each(lambda x, m: x - m, y, mean)
        var = each(lambda x: head_sum(x * x) * (1.0 / HEAD_DIM), yc)
        yn = each(lambda x, vr: x * lax.rsqrt(vr + RWKV_GN_EPS) * lnw_ref[...] + lnb_ref[...], yc, var)
        bonus = each(lambda rr, kx, vv: head_sum(rr * kx * rk_ref[...]) * vv, r, k2, v)
        for s in seqs:
            o_ref[s, rows, :] = (yn[s] + bonus[s]) * g[s]
        return 0

    lax.fori_loop(0, tb // L, chunks, 0)


def rwkv_mixer(p_rwkv, mu, w0, w2, a0, a2, g2, k_k, k_a, r_k, ln_w, ln_b, bsz, seq, tb=256, nb=8):
    n_tok = p_rwkv.shape[0]
    nb = nb if bsz % nb == 0 else 1
    row = lambda t: t.reshape(1, -1).astype(F32)
    w2p = jnp.concatenate([w2, jnp.zeros_like(a2)], axis=0).astype(BF16)
    a2p = jnp.concatenate([jnp.zeros_like(w2), a2], axis=0).astype(BF16)
    head_id = jnp.arange(GROUP_WIDTH) // HEAD_DIM
    hsum = (head_id[:, None] == head_id[None, :]).astype(BF16)
    const = lambda shape: pl.BlockSpec(shape, lambda b, j: (0,) * len(shape))
    vec = const((1, GROUP_WIDTH))
    return pl.pallas_call(
        _rwkv_body,
        grid=(bsz // nb, seq // tb),
        in_specs=[pl.BlockSpec((nb, tb, RWKV_SEG), lambda b, j: (b, j, 0)),
                  const((1, RWKV_SEG)), vec, const((128, GROUP_WIDTH)), vec, const((128, GROUP_WIDTH)),
                  const((128, GROUP_WIDTH)), vec, vec, vec, vec, vec, const((GROUP_WIDTH, GROUP_WIDTH))],
        out_specs=pl.BlockSpec((nb, tb, GROUP_WIDTH), lambda b, j: (b, j, 0)),
        out_shape=jax.ShapeDtypeStruct((bsz, seq, GROUP_WIDTH), F32),
        scratch_shapes=[pltpu.VMEM((nb, 1, RWKV_SEG), F32), pltpu.VMEM((nb, RWKV_MAT, RWKV_MAT), F32)],
        compiler_params=_params(("parallel", "arbitrary")),
        name="rwkv_mixer",
    )(p_rwkv.reshape(bsz, seq, RWKV_SEG), row(mu), row(w0), w2p, row(a0), a2p, g2.astype(BF16), row(k_k), row(k_a),
      row(r_k), row(ln_w), row(ln_b), hsum).reshape(n_tok, GROUP_WIDTH)


N_EXPERTS = 8
TOP_K = 2
MOE_ROWS = 256
ROUTE_LANES = 128
FF_CHUNK = 256


def _router_body(x_ref, g_ref, w_ref, b_ref, o_ref):
    h = _rms(x_ref[...], g_ref[...])
    logits = jnp.dot(h, w_ref[...], preferred_element_type=F32, precision=HIGHEST) + b_ref[...]
    lane = lax.broadcasted_iota(jnp.int32, logits.shape, 1)
    m1 = jnp.max(logits, axis=-1, keepdims=True)
    i1 = jnp.min(jnp.where(logits == m1, lane, ROUTE_LANES), axis=-1, keepdims=True)
    rest = jnp.where(lane == i1, MASKED, logits)
    m2 = jnp.max(rest, axis=-1, keepdims=True)
    i2 = jnp.min(jnp.where(rest == m2, lane, ROUTE_LANES), axis=-1, keepdims=True)
    e2 = jnp.exp(m2 - m1)
    g1 = 1.0 / (1.0 + e2)
    out = jnp.where(lane == 0, i1.astype(F32), jnp.where(lane == 1, i2.astype(F32),
                    jnp.where(lane == 2, g1, jnp.where(lane == 3, e2 * g1, 0.0))))
    o_ref[...] = out


def moe_router(x2, gain, router_w, router_b, tm=512):
    n_tok = x2.shape[0]
    w = jnp.pad(router_w.astype(F32), ((0, 0), (0, ROUTE_LANES - N_EXPERTS)))
    b = jnp.pad(router_b.astype(F32), (0, ROUTE_LANES - N_EXPERTS), constant_values=MASKED).reshape(1, ROUTE_LANES)
    return pl.pallas_call(
        _router_body,
        grid=(n_tok // tm,),
        in_specs=[pl.BlockSpec((tm, D_MODEL), lambda i: (i, 0)),
                  pl.BlockSpec((1, D_MODEL), lambda i: (0, 0)),
                  pl.BlockSpec((D_MODEL, ROUTE_LANES), lambda i: (0, 0)),
                  pl.BlockSpec((1, ROUTE_LANES), lambda i: (0, 0))],
        out_specs=pl.BlockSpec((tm, ROUTE_LANES), lambda i: (i, 0)),
        out_shape=jax.ShapeDtypeStruct((n_tok, ROUTE_LANES), F32),
        compiler_params=_params(("parallel",)),
        name="moe_router",
    )(x2, gain.reshape(1, D_MODEL), w, b)


def _moe_dispatch(experts, n_tok):
    n_assign = n_tok * TOP_K
    n_blocks = n_assign // MOE_ROWS + N_EXPERTS
    cap = n_blocks * MOE_ROWS
    flat_exp = experts.reshape(-1)
    onehot = (flat_exp[:, None] == jnp.arange(N_EXPERTS, dtype=jnp.int32)[None, :]).astype(jnp.int32)
    csum = jnp.cumsum(onehot, axis=0)
    counts = csum[-1]
    rank = jnp.sum(onehot * csum, axis=1) - 1
    padded = (counts + MOE_ROWS - 1) // MOE_ROWS * MOE_ROWS
    pad_end = jnp.cumsum(padded)
    pad_start = pad_end - padded
    dest = pad_start[flat_exp] + rank
    slot_asg = jnp.full((cap,), -1, jnp.int32).at[dest].set(jnp.arange(n_assign, dtype=jnp.int32))
    valid = slot_asg >= 0
    slot_tok = jnp.where(valid, slot_asg // TOP_K, 0)
    spare = n_assign + jnp.cumsum((~valid).astype(jnp.int32)) - 1
    slot_dst = jnp.where(valid, (slot_asg % TOP_K) * n_tok + slot_asg // TOP_K, spare)
    block_exp = jnp.minimum(jnp.searchsorted(pad_end, jnp.arange(n_blocks, dtype=jnp.int32) * MOE_ROWS, side='right'),
                            N_EXPERTS - 1).astype(jnp.int32)
    return (slot_tok.reshape(n_blocks, 1, MOE_ROWS), slot_dst.reshape(n_blocks, 1, MOE_ROWS), block_exp)


def _expert_body(bexp_ref, tok0_ref, tok_next_ref, dst_prev_ref, dst_cur_ref, x_hbm, g_ref, wg_ref, wu_ref, wd_ref,
                 y_hbm, xbuf, obuf, act_scr, sem_in, sem_out):
    del bexp_ref
    i = pl.program_id(0)
    last = pl.num_programs(0) - 1
    cur = i % 2
    nxt = 1 - cur
    n_chunks = D_FF // FF_CHUNK

    def gather_row(tok_ref, r, slot):
        return pltpu.make_async_copy(x_hbm.at[pl.ds(tok_ref[0, 0, r], 1)], xbuf.at[slot, pl.ds(r, 1)], sem_in.at[slot])

    def scatter_row(dst_ref, r, slot):
        return pltpu.make_async_copy(obuf.at[slot, pl.ds(r, 1)], y_hbm.at[pl.ds(dst_ref[0, 0, r], 1)], sem_out.at[slot])

    def wait_gather(slot):
        pltpu.make_async_copy(x_hbm.at[pl.ds(0, MOE_ROWS)], xbuf.at[slot], sem_in.at[slot]).wait()

    def wait_scatter(slot):
        pltpu.make_async_copy(obuf.at[slot], y_hbm.at[pl.ds(0, MOE_ROWS)], sem_out.at[slot]).wait()

    @pl.when(i == 0)
    def _():
        def start(r, _):
            gather_row(tok0_ref, r, 0).start()
            return 0
        lax.fori_loop(0, MOE_ROWS, start, 0)
        obuf[1] = jnp.zeros((MOE_ROWS, D_MODEL), F32)

    wait_gather(cur)
    h = _rms(xbuf[cur], g_ref[...]).astype(BF16)
    out_chunks = D_MODEL // FF_CHUNK
    rows_per_stage = -(-MOE_ROWS // (n_chunks + out_chunks))

    def start_rows(stage):
        for r in range(stage * rows_per_stage, min((stage + 1) * rows_per_stage, MOE_ROWS)):
            gather_row(tok_next_ref, r, nxt).start()
            scatter_row(dst_prev_ref, r, nxt).start()

    for j in range(n_chunks):
        start_rows(j)
        cols = slice(j * FF_CHUNK, (j + 1) * FF_CHUNK)
        gate = jnp.dot(h, wg_ref[0, :, cols], preferred_element_type=F32)
        up = jnp.dot(h, wu_ref[0, :, cols], preferred_element_type=F32)
        act_scr[:, cols] = (gate * jax.nn.sigmoid(gate) * up).astype(BF16)

    @pl.when(i > 0)
    def _():
        wait_scatter(cur)

    act = act_scr[...]
    for n in range(out_chunks):
        start_rows(n_chunks + n)
        cols = slice(n * FF_CHUNK, (n + 1) * FF_CHUNK)
        obuf[cur, :, cols] = jnp.dot(act, wd_ref[0, :, cols], preferred_element_type=F32)

    @pl.when(i == last)
    def _():
        def start(r, _):
            scatter_row(dst_cur_ref, r, cur).start()
            return 0
        lax.fori_loop(0, MOE_ROWS, start, 0)
        wait_scatter(nxt)
        wait_scatter(cur)
        wait_gather(nxt)


def moe_experts(x2, gain, slot_tok, slot_dst, block_exp, wg, wu, wd):
    n_blocks = block_exp.shape[0]
    cap = n_blocks * MOE_ROWS
    spare = (cap + jnp.arange(MOE_ROWS, dtype=jnp.int32)).reshape(1, 1, MOE_ROWS)
    dst_ext = jnp.concatenate([spare, slot_dst], axis=0)
    smem = lambda index: pl.BlockSpec((1, 1, MOE_ROWS), index, memory_space=pltpu.SMEM)
    grid_spec = pltpu.PrefetchScalarGridSpec(
        num_scalar_prefetch=1,
        grid=(n_blocks,),
        in_specs=[smem(lambda i, be: (0, 0, 0)),
                  smem(lambda i, be: (jnp.minimum(i + 1, n_blocks - 1), 0, 0)),
                  smem(lambda i, be: (i, 0, 0)),
                  smem(lambda i, be: (i + 1, 0, 0)),
                  pl.BlockSpec(memory_space=pl.ANY),
                  pl.BlockSpec((1, D_MODEL), lambda i, be: (0, 0)),
                  pl.BlockSpec((1, D_MODEL, D_FF), lambda i, be: (be[i], 0, 0)),
                  pl.BlockSpec((1, D_MODEL, D_FF), lambda i, be: (be[i], 0, 0)),
                  pl.BlockSpec((1, D_FF, D_MODEL), lambda i, be: (be[i], 0, 0))],
        out_specs=pl.BlockSpec(memory_space=pl.ANY),
        scratch_shapes=[pltpu.VMEM((2, MOE_ROWS, D_MODEL), F32), pltpu.VMEM((2, MOE_ROWS, D_MODEL), F32),
                        pltpu.VMEM((MOE_ROWS, D_FF), BF16),
                        pltpu.SemaphoreType.DMA((2,)), pltpu.SemaphoreType.DMA((2,))],
    )
    return pl.pallas_call(
        _expert_body,
        grid_spec=grid_spec,
        out_shape=jax.ShapeDtypeStruct((cap + MOE_ROWS, D_MODEL), F32),
        compiler_params=_params(("arbitrary",)),
        name="moe_experts",
    )(block_exp, slot_tok, slot_tok, dst_ext, dst_ext, x2, gain.reshape(1, D_MODEL), wg, wu, wd)


def _combine_body(x_ref, y1_ref, y2_ref, r_ref, o_ref):
    route = r_ref[...]
    o_ref[...] = x_ref[...] + route[:, 2:3] * y1_ref[...] + route[:, 3:4] * y2_ref[...]


def moe_combine(x2, y, route, tm=512):
    n_tok = x2.shape[0]
    tiles = n_tok // tm
    return pl.pallas_call(
        _combine_body,
        grid=(tiles,),
        in_specs=[pl.BlockSpec((tm, D_MODEL), lambda i: (i, 0)),
                  pl.BlockSpec((tm, D_MODEL), lambda i: (i, 0)),
                  pl.BlockSpec((tm, D_MODEL), lambda i: (tiles + i, 0)),
                  pl.BlockSpec((tm, ROUTE_LANES), lambda i: (i, 0))],
        out_specs=pl.BlockSpec((tm, D_MODEL), lambda i: (i, 0)),
        out_shape=jax.ShapeDtypeStruct((n_tok, D_MODEL), F32),
        compiler_params=_params(("parallel",)),
        name="moe_combine",
    )(x2, y, y, route)


def moe_ffn(x2, gain, router_w, router_b, wg, wu, wd):
    n_tok = x2.shape[0]
    route = moe_router(x2, gain, router_w, router_b)
    experts = route[:, :TOP_K].astype(jnp.int32)
    slot_tok, slot_dst, block_exp = _moe_dispatch(experts, n_tok)
    y = moe_experts(x2, gain, slot_tok, slot_dst, block_exp, wg, wu, wd)
    return moe_combine(x2, y, route)


def kernel(x, ln_mix, w_in, w_out, ssd_conv_w, ssd_conv_b, ssd_dt_bias, ssd_a_log, ssd_d, ssd_norm, att_q_norm, att_k_norm, att_out_norm, rwkv_mu, rwkv_w0, rwkv_w2, rwkv_a0, rwkv_a2, rwkv_g2, rwkv_k_k, rwkv_k_a, rwkv_r_k, rwkv_ln_w, rwkv_ln_b, s5_a_re, s5_a_im, s5_b_re, s5_b_im, s5_c_re, s5_c_im, s5_log_dt, s5_d, s5_glu_w, s5_glu_b, s5_out_norm, ln_ffn, ffn_w_gate, ffn_w_up, ffn_w_down, moe_router_w, moe_router_b, moe_w_gate, moe_w_up, moe_w_down):
    bsz, seq, dm = x.shape
    n_tok = bsz * seq
    depth = ln_mix.shape[0]
    x2 = x.reshape(n_tok, dm)
    for layer in range(depth):
        w_seg = _segment_w_in(w_in[layer])
        p_ssd, p_att, p_rwkv, p_s5 = inproj(x2, ln_mix[layer], w_seg, bsz, seq)
        y_a = ssd_mixer(p_ssd, ssd_conv_w[layer], ssd_conv_b[layer], ssd_dt_bias[layer], ssd_a_log[layer],
                        ssd_d[layer], ssd_norm[layer], bsz, seq)
        y_b = attention_mixer(p_att, att_q_norm[layer], att_k_norm[layer], att_out_norm[layer], bsz, seq)
        y_c = rwkv_mixer(p_rwkv, rwkv_mu[layer], rwkv_w0[layer], rwkv_w2[layer], rwkv_a0[layer], rwkv_a2[layer],
                         rwkv_g2[layer], rwkv_k_k[layer], rwkv_k_a[layer], rwkv_r_k[layer], rwkv_ln_w[layer],
                         rwkv_ln_b[layer], bsz, seq)
        y_d = s5_mixer(p_s5, s5_a_re[layer], s5_a_im[layer], s5_b_re[layer], s5_b_im[layer], s5_c_re[layer],
                       s5_c_im[layer], s5_log_dt[layer], s5_d[layer], s5_glu_w[layer], s5_glu_b[layer],
                       s5_out_norm[layer], bsz, seq)
        y_d_tm = y_d.reshape(seq, bsz * GROUP_WIDTH)
        x2 = outproj(x2, y_a, y_b, y_c, y_d_tm, w_out[layer].astype(BF16), bsz, seq)
        idx = layer // 2
        if layer % 2 == 0:
            x2 = dense_ffn(x2, ln_ffn[layer], ffn_w_gate[idx].astype(BF16), ffn_w_up[idx].astype(BF16),
                           ffn_w_down[idx].astype(BF16))
        else:
            x2 = moe_ffn(x2, ln_ffn[layer], moe_router_w[idx], moe_router_b[idx], moe_w_gate[idx].astype(BF16),
                         moe_w_up[idx].astype(BF16), moe_w_down[idx].astype(BF16))
    return x2.reshape(bsz, seq, dm)
```

```python
import functools
import math

import jax
import jax.numpy as jnp
from jax import lax
from jax.experimental import pallas as pl
from jax.experimental.pallas import tpu as pltpu

F32 = jnp.float32
BF16 = jnp.bfloat16

D_MODEL = 1024
GROUP_WIDTH = 256
HEAD_DIM = 64
NORM_EPS = 1e-6
D_FF = 2816

SSD_XBC = 512
SSD_SEG = 1024
ATT_SEG = 768
RWKV_SEG = 1024
S5_SEG = 256
SEG_WIDTHS = (SSD_SEG, ATT_SEG, RWKV_SEG, S5_SEG)
SEG_TOTAL = sum(SEG_WIDTHS)

VMEM_LIMIT = 56 * 1024 * 1024


def _params(sem):
    return pltpu.CompilerParams(dimension_semantics=sem, vmem_limit_bytes=VMEM_LIMIT)


def _rms(x, gain):
    ms = jnp.mean(x * x, axis=-1, keepdims=True)
    return x * lax.rsqrt(ms + NORM_EPS) * gain


def _inproj_body(x_ref, g_ref, w_ref, o_ssd, o_att, o_rwkv, o_s5):
    h = _rms(x_ref[...], g_ref[...]).astype(BF16)
    off = 0
    for o_ref in (o_ssd, o_att, o_rwkv, o_s5):
        n = o_ref.shape[-1]
        o_ref[...] = jnp.dot(h, w_ref[:, off:off + n], preferred_element_type=F32)
        off += n


def inproj(x2, gain, w_seg, bsz, seq, tm=512):
    n_tok = x2.shape[0]
    tiles_per_seq = seq // tm
    row = lambda i: (i, 0)
    return pl.pallas_call(
        _inproj_body,
        grid=(n_tok // tm,),
        in_specs=[pl.BlockSpec((tm, D_MODEL), row),
                  pl.BlockSpec((1, D_MODEL), lambda i: (0, 0)),
                  pl.BlockSpec((D_MODEL, SEG_TOTAL), lambda i: (0, 0))],
        out_specs=[pl.BlockSpec((tm, SSD_SEG), row),
                   pl.BlockSpec((tm, ATT_SEG), row),
                   pl.BlockSpec((tm, RWKV_SEG), row),
                   pl.BlockSpec((tm, S5_SEG), lambda i: (i % tiles_per_seq, i // tiles_per_seq))],
        out_shape=[jax.ShapeDtypeStruct((n_tok, SSD_SEG), F32),
                   jax.ShapeDtypeStruct((n_tok, ATT_SEG), F32),
                   jax.ShapeDtypeStruct((n_tok, RWKV_SEG), F32),
                   jax.ShapeDtypeStruct((seq, bsz * S5_SEG), F32)],
        compiler_params=_params(("parallel",)),
        name="inproj",
    )(x2, gain.reshape(1, D_MODEL), w_seg)


def _segment_w_in(w):
    dt_cols = jnp.repeat(w[:, 768:772], HEAD_DIM, axis=1)
    return jnp.concatenate([w[:, :768], dt_cols, w[:, 772:]], axis=1).astype(BF16)


def _outproj_body(x_ref, ya_ref, yb_ref, yc_ref, yd_ref, w_ref, o_ref):
    acc = x_ref[...]
    for j, y_ref in enumerate((ya_ref, yb_ref, yc_ref, yd_ref)):
        acc = acc + jnp.dot(y_ref[...].astype(BF16), w_ref[j * GROUP_WIDTH:(j + 1) * GROUP_WIDTH, :],
                            preferred_element_type=F32)
    o_ref[...] = acc


def outproj(x2, ya, yb, yc, yd_tm, w_out_bf, bsz, seq, tm=512):
    n_tok = x2.shape[0]
    tiles_per_seq = seq // tm
    row = lambda i: (i, 0)
    return pl.pallas_call(
        _outproj_body,
        grid=(n_tok // tm,),
        in_specs=[pl.BlockSpec((tm, D_MODEL), row),
                  pl.BlockSpec((tm, GROUP_WIDTH), row),
                  pl.BlockSpec((tm, GROUP_WIDTH), row),
                  pl.BlockSpec((tm, GROUP_WIDTH), row),
                  pl.BlockSpec((tm, GROUP_WIDTH), lambda i: (i % tiles_per_seq, i // tiles_per_seq)),
                  pl.BlockSpec((D_MODEL, D_MODEL), lambda i: (0, 0))],
        out_specs=pl.BlockSpec((tm, D_MODEL), row),
        out_shape=jax.ShapeDtypeStruct((n_tok, D_MODEL), F32),
        compiler_params=_params(("parallel",)),
        name="outproj",
    )(x2, ya, yb, yc, yd_tm, w_out_bf)


FFN_COLS = 256


def _ffn_body(x_ref, g_ref, wg_ref, wu_ref, wd_ref, o_ref, act_scr):
    h = _rms(x_ref[...], g_ref[...]).astype(BF16)
    for j in range(D_FF // FFN_COLS):
        cols = slice(j * FFN_COLS, (j + 1) * FFN_COLS)
        gate = jnp.dot(h, wg_ref[:, cols], preferred_element_type=F32)
        up = jnp.dot(h, wu_ref[:, cols], preferred_element_type=F32)
        act_scr[:, cols] = (gate * jax.nn.sigmoid(gate) * up).astype(BF16)
    act = act_scr[...]
    for n in range(D_MODEL // FFN_COLS):
        cols = slice(n * FFN_COLS, (n + 1) * FFN_COLS)
        o_ref[:, cols] = x_ref[:, cols] + jnp.dot(act, wd_ref[:, cols], preferred_element_type=F32)


def dense_ffn(x2, gain, wg, wu, wd, tm=512):
    n_tok = x2.shape[0]
    resident = lambda shape: pl.BlockSpec(shape, lambda i: (0, 0), pipeline_mode=pl.Buffered(1))
    return pl.pallas_call(
        _ffn_body,
        grid=(n_tok // tm,),
        in_specs=[pl.BlockSpec((tm, D_MODEL), lambda i: (i, 0)),
                  pl.BlockSpec((1, D_MODEL), lambda i: (0, 0)),
                  resident((D_MODEL, D_FF)), resident((D_MODEL, D_FF)), resident((D_FF, D_MODEL))],
        out_specs=pl.BlockSpec((tm, D_MODEL), lambda i: (i, 0)),
        out_shape=jax.ShapeDtypeStruct((n_tok, D_MODEL), F32),
        scratch_shapes=[pltpu.VMEM((tm, D_FF), BF16)],
        compiler_params=_params(("parallel",)),
        name="dense_ffn",
    )(x2, gain.reshape(1, D_MODEL), wg, wu, wd)


S5_GROUPS = 16
S5_GROUP = 16
S5_STATE = 64
S5_CH = S5_GROUPS * S5_STATE
S5_BT = 8


def _s5_body(u_ref, wb_ref, wc_ref, are_ref, aim_ref, d_ref, gw_ref, gb_ref, gain_ref, o_ref, h_scr, st_scr):
    tt = u_ref.shape[0]

    @pl.when(pl.program_id(1) == 0)
    def _():
        st_scr[...] = jnp.zeros_like(st_scr)

    u2 = u_ref[...].reshape(tt * S5_BT, GROUP_WIDTH)
    bu = jnp.dot(u2.astype(BF16), wb_ref[...], preferred_element_type=F32)
    h_scr[...] = bu.reshape(tt, S5_BT, 2 * S5_CH)
    a_re = jnp.broadcast_to(are_ref[...], (S5_BT, S5_CH))
    a_im = jnp.broadcast_to(aim_ref[...], (S5_BT, S5_CH))

    def step(t, carry):
        h_re, h_im = carry
        n_re = a_re * h_re - a_im * h_im + h_scr[t, :, :S5_CH]
        n_im = a_re * h_im + a_im * h_re + h_scr[t, :, S5_CH:]
        h_scr[t, :, :S5_CH] = n_re
        h_scr[t, :, S5_CH:] = n_im
        return n_re, n_im

    h_re, h_im = lax.fori_loop(0, tt, step, (st_scr[0], st_scr[1]), unroll=4)
    st_scr[0] = h_re
    st_scr[1] = h_im

    hall = h_scr[...].reshape(tt * S5_BT, 2 * S5_CH)
    y = jnp.dot(hall.astype(BF16), wc_ref[...], preferred_element_type=F32) + d_ref[...] * u2
    hh = jax.nn.gelu(y)
    gl = jnp.dot(hh.astype(BF16), gw_ref[...], preferred_element_type=F32) + gb_ref[...]
    out = _rms(hh * jax.nn.sigmoid(gl), gain_ref[...])
    o_ref[...] = out.reshape(tt, S5_BT, GROUP_WIDTH)


def _s5_weights(a_re, a_im, b_re, b_im, c_re, c_im, log_dt):
    dt = jnp.exp(log_dt)[:, None]
    mag = jnp.exp(dt * a_re)
    ab_re = mag * jnp.cos(dt * a_im)
    ab_im = mag * jnp.sin(dt * a_im)
    den = a_re * a_re + a_im * a_im
    f_re = ((ab_re - 1.0) * a_re + ab_im * a_im) / den
    f_im = (ab_im * a_re - (ab_re - 1.0) * a_im) / den
    bb_re = f_re[..., None] * b_re - f_im[..., None] * b_im
    bb_im = f_re[..., None] * b_im + f_im[..., None] * b_re
    eye = jnp.eye(S5_GROUPS, dtype=F32)
    expand_b = lambda m: jnp.einsum('gpc,gh->gchp', m, eye).reshape(GROUP_WIDTH, S5_CH)
    expand_c = lambda m: jnp.einsum('gcp,gh->gphc', m, eye).reshape(S5_CH, GROUP_WIDTH)
    wb = jnp.concatenate([expand_b(bb_re), expand_b(bb_im)], axis=1).astype(BF16)
    wc = jnp.concatenate([expand_c(c_re), -expand_c(c_im)], axis=0).astype(BF16)
    return wb, wc, ab_re.reshape(1, S5_CH), ab_im.reshape(1, S5_CH)


def s5_mixer(u_tm, a_re, a_im, b_re, b_im, c_re, c_im, log_dt, d_skip, glu_w, glu_b, out_gain, bsz, seq, tt=128):
    wb, wc, are, aim = _s5_weights(a_re, a_im, b_re, b_im, c_re, c_im, log_dt)
    u3 = u_tm.reshape(seq, bsz, GROUP_WIDTH)
    const = lambda shape: pl.BlockSpec(shape, lambda i, j: (0,) * len(shape))
    row = lambda v: v.reshape(1, GROUP_WIDTH)
    return pl.pallas_call(
        _s5_body,
        grid=(bsz // S5_BT, seq // tt),
        in_specs=[pl.BlockSpec((tt, S5_BT, GROUP_WIDTH), lambda i, j: (j, i, 0)),
                  const((GROUP_WIDTH, 2 * S5_CH)), const((2 * S5_CH, GROUP_WIDTH)),
                  const((1, S5_CH)), const((1, S5_CH)), const((1, GROUP_WIDTH)),
                  const((GROUP_WIDTH, GROUP_WIDTH)), const((1, GROUP_WIDTH)), const((1, GROUP_WIDTH))],
        out_specs=pl.BlockSpec((tt, S5_BT, GROUP_WIDTH), lambda i, j: (j, i, 0)),
        out_shape=jax.ShapeDtypeStruct((seq, bsz, GROUP_WIDTH), F32),
        scratch_shapes=[pltpu.VMEM((tt, S5_BT, 2 * S5_CH), F32), pltpu.VMEM((2, S5_BT, S5_CH), F32)],
        compiler_params=_params(("parallel", "arbitrary")),
        name="s5_mixer",
    )(u3, wb, wc, are, aim, row(d_skip), glu_w.astype(BF16), row(glu_b), row(out_gain))


SSD_CHUNK = 128
SSD_CONV = 4
SSD_HEADS = 4
SSD_TAIL = 8
HIGHEST = lax.Precision.HIGHEST


def _dot_nt(a, b, **kw):
    return lax.dot_general(a, b, (((1,), (1,)), ((), ())), preferred_element_type=F32, **kw)


def _dot_tn(a, b, **kw):
    return lax.dot_general(a, b, (((0,), (0,)), ((), ())), preferred_element_type=F32, **kw)


def _ssd_body(p_ref, cw_ref, cb_ref, dtb_ref, arow_ref, drow_ref, gain_ref, o_ref, tail_scr, st_scr):
    tb = p_ref.shape[0]
    L = SSD_CHUNK

    @pl.when(pl.program_id(1) == 0)
    def _():
        tail_scr[...] = jnp.zeros_like(tail_scr)
        st_scr[...] = jnp.zeros_like(st_scr)

    raw = p_ref[:, :SSD_XBC]
    ext = jnp.concatenate([tail_scr[...], raw], axis=0)
    conv = cb_ref[...] + cw_ref[SSD_CONV - 1:SSD_CONV, :] * raw
    for j in range(SSD_CONV - 1):
        shifted = pltpu.roll(ext, SSD_CONV - 1 - j, axis=0)[SSD_TAIL:, :]
        conv = conv + cw_ref[j:j + 1, :] * shifted
    tail_scr[...] = raw[tb - SSD_TAIL:, :]
    xc = conv * jax.nn.sigmoid(conv)

    dt = jax.nn.softplus(p_ref[:, 768:1024] + dtb_ref[...])
    a = dt * arow_ref[...]

    ii = lax.broadcasted_iota(jnp.int32, (L, L), 0)
    jj = lax.broadcasted_iota(jnp.int32, (L, L), 1)
    causal = ii >= jj
    tril = causal.astype(BF16)
    low_half = jj < HEAD_DIM
    lane2 = lax.broadcasted_iota(jnp.int32, (L, GROUP_WIDTH), 1)
    lane1 = lax.broadcasted_iota(jnp.int32, (L, L), 1)
    row_grp = lax.broadcasted_iota(jnp.int32, (L, GROUP_WIDTH), 0) // HEAD_DIM
    state_mask = row_grp == lane2 // (2 * HEAD_DIM)

    for c in range(tb // L):
        sl = slice(c * L, (c + 1) * L)
        xs = xc[sl, :GROUP_WIDTH]
        bm = xc[sl, GROUP_WIDTH:GROUP_WIDTH + L]
        cm = xc[sl, GROUP_WIDTH + L:]
        xdt = xs * dt[sl]
        a_hi = a[sl].astype(BF16)
        a_lo = (a[sl] - a_hi.astype(F32)).astype(BF16)
        cs = (jnp.dot(tril, a_hi, preferred_element_type=F32)
              + jnp.dot(tril, a_lo, preferred_element_type=F32))
        cs_end = cs[L - 1:L, :]
        st = st_scr[...]
        y = jnp.exp(cs) * jnp.dot(cm.astype(BF16), st.astype(BF16), preferred_element_type=F32)
        for g in range(2):
            cg = jnp.where((lane1 // HEAD_DIM) == g, cm, 0.0)
            gram = _dot_nt(cg.astype(BF16), bm.astype(BF16))
            pair = cs[:, g * L:(g + 1) * L]
            swapped = pltpu.roll(pair, HEAD_DIM, axis=1)
            for k in range(2):
                h = 2 * g + k
                col = jnp.where(low_half, pair, swapped) if k == 0 else jnp.where(low_half, swapped, pair)
                seg = col - col.T
                decay = jnp.where(causal, jnp.exp(jnp.minimum(seg, 0.0)), 0.0)
                yd = jnp.dot((gram * decay).astype(BF16), xdt.astype(BF16), preferred_element_type=F32)
                y = y + jnp.where((lane2 // HEAD_DIM) == h, yd, 0.0)
        upd = _dot_tn(bm.astype(BF16), (xdt * jnp.exp(cs_end - cs)).astype(BF16))
        st_scr[...] = st * jnp.exp(cs_end) + jnp.where(state_mask, upd, 0.0)
        y = y + drow_ref[...] * xs
        z = p_ref[sl, 512:768]
        o_ref[sl, :] = _rms(y * (z * jax.nn.sigmoid(z)), gain_ref[...])


def ssd_mixer(p_ssd, conv_w, conv_b, dt_bias, a_log, d_skip, norm_gain, bsz, seq, tb=512):
    n_tok = p_ssd.shape[0]
    blocks = seq // tb
    per_head = lambda v: jnp.repeat(v.astype(F32), HEAD_DIM).reshape(1, GROUP_WIDTH)
    const = lambda shape: pl.BlockSpec(shape, lambda b, j: (0,) * len(shape))
    return pl.pallas_call(
        _ssd_body,
        grid=(bsz, blocks),
        in_specs=[pl.BlockSpec((tb, SSD_SEG), lambda b, j: (b * blocks + j, 0)),
                  const((SSD_CONV, SSD_XBC)), const((1, SSD_XBC)), const((1, GROUP_WIDTH)),
                  const((1, GROUP_WIDTH)), const((1, GROUP_WIDTH)), const((1, GROUP_WIDTH))],
        out_specs=pl.BlockSpec((tb, GROUP_WIDTH), lambda b, j: (b * blocks + j, 0)),
        out_shape=jax.ShapeDtypeStruct((n_tok, GROUP_WIDTH), F32),
        scratch_shapes=[pltpu.VMEM((SSD_TAIL, SSD_XBC), F32), pltpu.VMEM((2 * HEAD_DIM, GROUP_WIDTH), F32)],
        compiler_params=_params(("parallel", "arbitrary")),
        name="ssd_mixer",
    )(p_ssd, conv_w, conv_b.reshape(1, SSD_XBC), per_head(dt_bias), per_head(-jnp.exp(a_log)), per_head(d_skip),
      norm_gain.reshape(1, GROUP_WIDTH))


ATT_HEADS = 4
ATT_BLOCK = 128
DILATED_PATTERNS = ((128, 1), (512, 4), (2048, 16))
MASKED = -1e30
ATT_PAR = 4


def _att_bias():
    slopes = [2.0 ** (-8.0 / ATT_HEADS * (h + 1)) for h in range(ATT_HEADS)]
    q_idx = jnp.arange(ATT_BLOCK)[:, None]
    k_idx = jnp.arange(2 * ATT_BLOCK)[None, :] - ATT_BLOCK
    rel = q_idx - k_idx
    out = []
    for window, dilation in DILATED_PATTERNS:
        span = window // dilation
        valid = (rel >= 0) & (rel <= span)
        per_head = [jnp.where(valid, -s * (rel * dilation).astype(F32), MASKED) for s in slopes]
        out.append(jnp.stack(per_head))
    return jnp.stack(out)


def _ld2(scr, rows):
    return jnp.concatenate([scr[0, rows, :], scr[1, rows, :]], axis=1)


def _st2(scr, rows, val):
    scr[0, rows, :] = val[:, :ATT_BLOCK]
    scr[1, rows, :] = val[:, ATT_BLOCK:]


def _att_body(qkv_ref, gmat_ref, qg_ref, kg_ref, og_ref, bias_ref, o_ref, q_scr, k_scr, v_scr, u_scr, m_scr, s_scr):
    seq = qkv_ref.shape[0]
    blk = ATT_BLOCK
    lane = lax.broadcasted_iota(jnp.int32, (blk, GROUP_WIDTH), 1) // HEAD_DIM
    every = pl.ds(0, seq)

    def head_norm(t, gain):
        sq = t * t
        hi = sq.astype(BF16)
        lo = (sq - hi.astype(F32)).astype(BF16)
        ms = (jnp.dot(hi, gmat_ref[...], preferred_element_type=F32)
              + jnp.dot(lo, gmat_ref[...], preferred_element_type=F32))
        return t * lax.rsqrt(ms + NORM_EPS) * gain

    _st2(q_scr, every, head_norm(qkv_ref[:, :GROUP_WIDTH], qg_ref[...]) * (1.0 / math.sqrt(HEAD_DIM)))
    _st2(k_scr, every, head_norm(qkv_ref[:, GROUP_WIDTH:2 * GROUP_WIDTH], kg_ref[...]))
    _st2(v_scr, every, qkv_ref[:, 2 * GROUP_WIDTH:])

    prev_cols = lax.broadcasted_iota(jnp.int32, (1, 2 * blk), 1) < blk
    heads = range(ATT_HEADS)

    def blocks(pat, specs):
        pairs = [(b, h) for b in range(len(specs)) for h in heads]
        q, keys, vals, gate = [], [], [], []
        for rows, prev_rows, has_prev in specs:
            q.append(_ld2(q_scr, rows))
            if prev_rows is None:
                keys.append(_ld2(k_scr, rows).astype(BF16))
                vals.append(_ld2(v_scr, rows).astype(BF16))
                gate.append(None)
            else:
                keys.append(jnp.concatenate([_ld2(k_scr, prev_rows), _ld2(k_scr, rows)], axis=0).astype(BF16))
                vals.append(jnp.concatenate([_ld2(v_scr, prev_rows), _ld2(v_scr, rows)], axis=0).astype(BF16))
                gate.append(jnp.where(jnp.logical_or(has_prev, jnp.logical_not(prev_cols)), 0.0, MASKED))
        bias = {(b, h): (bias_ref[pat, h, :, blk:] if gate[b] is None else bias_ref[pat, h] + gate[b])
                for b, h in pairs}
        qh = {(b, h): jnp.where(lane == h, q[b], 0.0).astype(BF16) for b, h in pairs}
        sc = {(b, h): _dot_nt(qh[b, h], keys[b]) + bias[b, h] for b, h in pairs}
        m = {bh: jnp.max(sc[bh], axis=-1, keepdims=True) for bh in pairs}
        p = {bh: jnp.exp(sc[bh] - m[bh]) for bh in pairs}
        den = {bh: jnp.sum(p[bh], axis=-1, keepdims=True) for bh in pairs}
        u = {(b, h): jnp.dot(p[b, h].astype(BF16), vals[b], preferred_element_type=F32) for b, h in pairs}
        out = []
        for b in range(len(specs)):
            u_all, m_all, s_all = u[b, 0], m[b, 0], den[b, 0]
            for h in heads[1:]:
                sel = lane == h
                u_all = jnp.where(sel, u[b, h], u_all)
                m_all = jnp.where(sel, m[b, h], m_all)
                s_all = jnp.where(sel, den[b, h], s_all)
            out.append((u_all, m_all, s_all))
        return out

    def merge(rows, u, m, s):
        m0 = _ld2(m_scr, rows)
        m_new = jnp.maximum(m0, m)
        alpha = jnp.exp(m0 - m_new)
        beta = jnp.exp(m - m_new)
        return alpha * _ld2(u_scr, rows) + beta * u, m_new, alpha * _ld2(s_scr, rows) + beta * s

    def save(rows, u, m, s):
        _st2(u_scr, rows, u)
        _st2(m_scr, rows, m)
        _st2(s_scr, rows, s)

    par = ATT_PAR
    n_iter = 16 // par

    def body16(i, _):
        specs = [(pl.ds(i + n_iter * b, blk, stride=16), None, None) for b in range(par)]
        for spec, res in zip(specs, blocks(2, specs)):
            save(spec[0], *res)
        return 0
    lax.fori_loop(0, n_iter, body16, 0)

    def body4(i, _):
        specs = []
        for b in range(par):
            idx = i + n_iter * b
            r = idx % 4
            n = idx // 4
            specs.append((pl.ds(n * (4 * blk) + r, blk, stride=4),
                          pl.ds(jnp.maximum(n - 1, 0) * (4 * blk) + r, blk, stride=4), n > 0))
        for spec, res in zip(specs, blocks(1, specs)):
            save(spec[0], *merge(spec[0], *res))
        return 0
    lax.fori_loop(0, n_iter, body4, 0)

    def body1(i, _):
        specs = []
        for b in range(par):
            n = i + (seq // blk // par) * b
            specs.append((pl.ds(pl.multiple_of(n * blk, blk), blk),
                          pl.ds(pl.multiple_of(jnp.maximum(n - 1, 0) * blk, blk), blk), n > 0))
        for spec, res in zip(specs, blocks(0, specs)):
            u, m, s = merge(spec[0], *res)
            o_ref[spec[0], :] = _rms(u / s, og_ref[...])
        return 0
    lax.fori_loop(0, seq // blk // par, body1, 0)


def attention_mixer(qkv, q_gain, k_gain, out_gain, bsz, seq):
    n_tok = qkv.shape[0]
    head_id = jnp.arange(GROUP_WIDTH) // HEAD_DIM
    gmat = ((head_id[:, None] == head_id[None, :]).astype(F32) / HEAD_DIM).astype(BF16)
    tile = lambda g: jnp.tile(g.astype(F32), ATT_HEADS).reshape(1, GROUP_WIDTH)
    const = lambda shape: pl.BlockSpec(shape, lambda b: (0,) * len(shape))
    return pl.pallas_call(
        _att_body,
        grid=(bsz,),
        in_specs=[pl.BlockSpec((seq, ATT_SEG), lambda b: (b, 0)),
                  const((GROUP_WIDTH, GROUP_WIDTH)), const((1, GROUP_WIDTH)), const((1, GROUP_WIDTH)),
                  const((1, GROUP_WIDTH)), const((3, ATT_HEADS, ATT_BLOCK, 2 * ATT_BLOCK))],
        out_specs=pl.BlockSpec((seq, GROUP_WIDTH), lambda b: (b, 0)),
        out_shape=jax.ShapeDtypeStruct((n_tok, GROUP_WIDTH), F32),
        scratch_shapes=[pltpu.VMEM((2, seq, ATT_BLOCK), F32) for _ in range(6)],
        compiler_params=_params(("parallel",)),
        name="attention_mixer",
    )(qkv, gmat, tile(q_gain), tile(k_gain), out_gain.reshape(1, GROUP_WIDTH), _att_bias())


RWKV_HEADS = 4
RWKV_CHUNK = 64
RWKV_DECAY_SCALE = 0.606531
RWKV_GN_EPS = 64e-5
RWKV_MAT = RWKV_HEADS * RWKV_CHUNK


def _per_head_blocks(x):
    lane_head = lax.broadcasted_iota(jnp.int32, x.shape, 1) // HEAD_DIM
    return jnp.concatenate([jnp.where(lane_head == h, x, 0.0) for h in range(RWKV_HEADS)], axis=0)


def _rwkv_body(p_ref, mu_ref, w0_ref, w2_ref, a0_ref, a2_ref, g2_ref, kk_ref, ka_ref, rk_ref, lnw_ref, lnb_ref,
               hsum_ref, o_ref, prev_scr, z_scr):
    nb, tb = p_ref.shape[0], p_ref.shape[1]
    L = RWKV_CHUNK

    def mb(a, b):
        return jnp.dot(a.astype(BF16), b.astype(BF16), preferred_element_type=F32)

    def split(x):
        hi = x.astype(BF16)
        return hi, (x - hi.astype(F32)).astype(BF16)

    def head_sum(x):
        return jnp.dot(x.astype(BF16), hsum_ref[...], preferred_element_type=F32)

    @pl.when(pl.program_id(1) == 0)
    def _():
        prev_scr[...] = jnp.zeros_like(prev_scr)
        z_scr[...] = jnp.zeros_like(z_scr)

    ii = lax.broadcasted_iota(jnp.int32, (L, RWKV_MAT), 0)
    jj = lax.broadcasted_iota(jnp.int32, (L, RWKV_MAT), 1) % L
    eye_all = (ii == jj).astype(F32)
    ti = lax.broadcasted_iota(jnp.int32, (L, L), 0)
    tj = lax.broadcasted_iota(jnp.int32, (L, L), 1)
    tril = (ti >= tj).astype(BF16)
    first_row = lax.broadcasted_iota(jnp.int32, (L, RWKV_SEG), 0) == 0
    zi = lax.broadcasted_iota(jnp.int32, (RWKV_MAT, RWKV_MAT), 0) // HEAD_DIM
    zj = lax.broadcasted_iota(jnp.int32, (RWKV_MAT, RWKV_MAT), 1) // HEAD_DIM
    same_head = zi == zj

    def each(fn, *lists):
        return [fn(*args) for args in zip(*lists)]

    def chunks(c, _):
        seqs = list(range(nb))
        rows = pl.ds(pl.multiple_of(c * L, L), L)
        pc = [p_ref[s, rows, :] for s in seqs]
        prev = [jnp.where(first_row, prev_scr[s], pltpu.roll(pc[s], 1, axis=0)) for s in seqs]
        for s in seqs:
            prev_scr[s] = pc[s][L - 1:L, :]
        pf = each(lambda x, xp: x + mu_ref[...] * (xp - x), pc, prev)
        r = [x[:, :GROUP_WIDTH] for x in pf]
        k = [x[:, GROUP_WIDTH:2 * GROUP_WIDTH] for x in pf]
        v = [x[:, 2 * GROUP_WIDTH:3 * GROUP_WIDTH] for x in pf]
        wa = [x[:, 3 * GROUP_WIDTH:3 * GROUP_WIDTH + 128] for x in pf]
        gd = [x[:, 3 * GROUP_WIDTH + 128:] for x in pf]
        log_w = each(lambda x: -RWKV_DECAY_SCALE * jax.nn.sigmoid(
            w0_ref[...] + jnp.dot(jnp.tanh(x).astype(BF16), w2_ref[...], preferred_element_type=F32)), wa)
        alr = each(lambda x: jax.nn.sigmoid(
            a0_ref[...] + jnp.dot(x.astype(BF16), a2_ref[...], preferred_element_type=F32)), wa)
        g = each(lambda x: jnp.dot(jax.nn.sigmoid(x).astype(BF16), g2_ref[...], preferred_element_type=F32), gd)
        kk = each(lambda x: x * kk_ref[...], k)
        kk_ss = each(lambda x: head_sum(x * x), kk)
        kk = each(lambda x, ss: x * lax.rsqrt(ss + 1e-12), kk, kk_ss)
        k2 = each(lambda x, a: x * (1.0 + (a - 1.0) * ka_ref[...]), k, alr)

        def cumsum(lw):
            hi, lo = split(lw)
            return (jnp.dot(tril, hi, preferred_element_type=F32) + jnp.dot(tril, lo, preferred_element_type=F32))
        cs = each(cumsum, log_w)
        gam = each(jnp.exp, cs)
        gam_inv = each(lambda x: jnp.exp(-x), cs)
        a_hat = each(lambda x, c_, lw: -x * jnp.exp(c_ - lw), kk, cs, log_w)
        b_hat = each(lambda x, a, gi: x * a * gi, kk, alr, gam_inv)
        k_hat = each(lambda x, gi: x * gi, k2, gam_inv)
        r_hat = each(lambda x, gm: x * gm, r, gam)

        gram = each(lambda a, rr, b, kh: _dot_nt(
            jnp.concatenate([a, rr], axis=0).astype(BF16),
            jnp.concatenate([_per_head_blocks(b), _per_head_blocks(kh)], axis=0).astype(BF16)),
            a_hat, r_hat, b_hat, k_hat)
        n_all = [jnp.where(ii > jj, x[:L, :RWKV_MAT], 0.0) for x in gram]
        m_all = [jnp.where(ii > jj, x[:L, RWKV_MAT:], 0.0) for x in gram]
        gb_all = [jnp.where(ii >= jj, x[L:, :RWKV_MAT], 0.0) for x in gram]
        gk_all = [jnp.where(ii >= jj, x[L:, RWKV_MAT:], 0.0) for x in gram]

        w_all = [eye_all + n for n in n_all]
        n_pow = each(lambda n: mb(n, _per_head_blocks(n)), n_all)
        for _ in range(4):
            res = each(lambda n, w: mb(jnp.concatenate([n, w], axis=0), _per_head_blocks(n)), n_pow, w_all)
            n_pow = [x[:L] for x in res]
            w_all = each(lambda w, x: w + x[L:], w_all, res)
        w_all = each(lambda w, n: w + mb(w, _per_head_blocks(n)), w_all, n_pow)

        t1 = each(lambda m, vv: mb(m, _per_head_blocks(vv)), m_all, v)
        at_vt = each(lambda w, a, t: mb(w, jnp.concatenate([_per_head_blocks(a), _per_head_blocks(t)], axis=1)),
                     w_all, a_hat, t1)
        a_til = [x[:, :RWKV_MAT] for x in at_vt]
        v_til = [x[:, RWKV_MAT:] for x in at_vt]

        z = [z_scr[s] for s in seqs]
        ax_rx = each(lambda a, rr, zz: _dot_nt(jnp.concatenate([a, rr], axis=0).astype(BF16), zz.astype(BF16)),
                     a_til, r_hat, z)
        u = each(lambda x, vt: x[:L] + vt, ax_rx, v_til)
        y = each(lambda x, gb, gk, uu, vv: x[L:] + mb(
            jnp.concatenate([gb, gk], axis=1),
            jnp.concatenate([_per_head_blocks(uu), _per_head_blocks(vv)], axis=0)), ax_rx, gb_all, gk_all, u, v)
        d = each(lambda uu, vv, b, kh: _dot_tn(jnp.concatenate([uu, vv], axis=0).astype(BF16),
                                               jnp.concatenate([b, kh], axis=0).astype(BF16)), u, v, b_hat, k_hat)
        for s in seqs:
            z_scr[s] = (z[s] + jnp.where(same_head, d[s], 0.0)) * gam[s][L - 1:L, :]

        mean = each(lambda x: head_sum(x) * (1.0 / HEAD_DIM), y)
        yc = each(lambda x, m: x - m, y, mean)
        var = each(lambda x: head_sum(x * x) * (1.0 / HEAD_DIM), yc)
        yn = each(lambda x, vr: x * lax.rsqrt(vr + RWKV_GN_EPS) * lnw_ref[...] + lnb_ref[...], yc, var)
        bonus = each(lambda rr, kx, vv: head_sum(rr * kx * rk_ref[...]) * vv, r, k2, v)
        for s in seqs:
            o_ref[s, rows, :] = (yn[s] + bonus[s]) * g[s]
        return 0

    lax.fori_loop(0, tb // L, chunks, 0)


def rwkv_mixer(p_rwkv, mu, w0, w2, a0, a2, g2, k_k, k_a, r_k, ln_w, ln_b, bsz, seq, tb=256, nb=8):
    n_tok = p_rwkv.shape[0]
    nb = nb if bsz % nb == 0 else 1
    row = lambda t: t.reshape(1, -1).astype(F32)
    w2p = jnp.concatenate([w2, jnp.zeros_like(a2)], axis=0).astype(BF16)
    a2p = jnp.concatenate([jnp.zeros_like(w2), a2], axis=0).astype(BF16)
    head_id = jnp.arange(GROUP_WIDTH) // HEAD_DIM
    hsum = (head_id[:, None] == head_id[None, :]).astype(BF16)
    const = lambda shape: pl.BlockSpec(shape, lambda b, j: (0,) * len(shape))
    vec = const((1, GROUP_WIDTH))
    return pl.pallas_call(
        _rwkv_body,
        grid=(bsz // nb, seq // tb),
        in_specs=[pl.BlockSpec((nb, tb, RWKV_SEG), lambda b, j: (b, j, 0)),
                  const((1, RWKV_SEG)), vec, const((128, GROUP_WIDTH)), vec, const((128, GROUP_WIDTH)),
                  const((128, GROUP_WIDTH)), vec, vec, vec, vec, vec, const((GROUP_WIDTH, GROUP_WIDTH))],
        out_specs=pl.BlockSpec((nb, tb, GROUP_WIDTH), lambda b, j: (b, j, 0)),
        out_shape=jax.ShapeDtypeStruct((bsz, seq, GROUP_WIDTH), F32),
        scratch_shapes=[pltpu.VMEM((nb, 1, RWKV_SEG), F32), pltpu.VMEM((nb, RWKV_MAT, RWKV_MAT), F32)],
        compiler_params=_params(("parallel", "arbitrary")),
        name="rwkv_mixer",
    )(p_rwkv.reshape(bsz, seq, RWKV_SEG), row(mu), row(w0), w2p, row(a0), a2p, g2.astype(BF16), row(k_k), row(k_a),
      row(r_k), row(ln_w), row(ln_b), hsum).reshape(n_tok, GROUP_WIDTH)


N_EXPERTS = 8
TOP_K = 2
MOE_ROWS = 256
ROUTE_LANES = 128
FF_CHUNK = 256
OUT_SLOTS = 3


def _router_body(x_ref, g_ref, w_ref, b_ref, o_ref):
    h = _rms(x_ref[...], g_ref[...])
    h_hi = h.astype(BF16)
    h_lo = (h - h_hi.astype(F32)).astype(BF16)
    both = jnp.dot(h_hi, w_ref[...], preferred_element_type=F32)
    logits = (both[:, :ROUTE_LANES] + both[:, ROUTE_LANES:]
              + jnp.dot(h_lo, w_ref[:, :ROUTE_LANES], preferred_element_type=F32) + b_ref[...])
    lane = lax.broadcasted_iota(jnp.int32, logits.shape, 1)
    m1 = jnp.max(logits, axis=-1, keepdims=True)
    i1 = jnp.min(jnp.where(logits == m1, lane, ROUTE_LANES), axis=-1, keepdims=True)
    rest = jnp.where(lane == i1, MASKED, logits)
    m2 = jnp.max(rest, axis=-1, keepdims=True)
    i2 = jnp.min(jnp.where(rest == m2, lane, ROUTE_LANES), axis=-1, keepdims=True)
    e2 = jnp.exp(m2 - m1)
    g1 = 1.0 / (1.0 + e2)
    out = jnp.where(lane == 0, i1.astype(F32), jnp.where(lane == 1, i2.astype(F32),
                    jnp.where(lane == 2, g1, jnp.where(lane == 3, e2 * g1, 0.0))))
    o_ref[...] = out


def moe_router(x2, gain, router_w, router_b, tm=512):
    n_tok = x2.shape[0]
    w = jnp.pad(router_w.astype(F32), ((0, 0), (0, ROUTE_LANES - N_EXPERTS)))
    w_hi = w.astype(BF16)
    w = jnp.concatenate([w_hi, (w - w_hi.astype(F32)).astype(BF16)], axis=1)
    b = jnp.pad(router_b.astype(F32), (0, ROUTE_LANES - N_EXPERTS), constant_values=MASKED).reshape(1, ROUTE_LANES)
    return pl.pallas_call(
        _router_body,
        grid=(n_tok // tm,),
        in_specs=[pl.BlockSpec((tm, D_MODEL), lambda i: (i, 0)),
                  pl.BlockSpec((1, D_MODEL), lambda i: (0, 0)),
                  pl.BlockSpec((D_MODEL, 2 * ROUTE_LANES), lambda i: (0, 0)),
                  pl.BlockSpec((1, ROUTE_LANES), lambda i: (0, 0))],
        out_specs=pl.BlockSpec((tm, ROUTE_LANES), lambda i: (i, 0)),
        out_shape=jax.ShapeDtypeStruct((n_tok, ROUTE_LANES), F32),
        compiler_params=_params(("parallel",)),
        name="moe_router",
    )(x2, gain.reshape(1, D_MODEL), w, b)


def _moe_dispatch(experts, n_tok):
    n_assign = n_tok * TOP_K
    n_blocks = n_assign // MOE_ROWS + N_EXPERTS
    cap = n_blocks * MOE_ROWS
    flat_exp = experts.reshape(-1)
    onehot = (flat_exp[:, None] == jnp.arange(N_EXPERTS, dtype=jnp.int32)[None, :]).astype(jnp.int32)
    csum = jnp.cumsum(onehot, axis=0)
    counts = csum[-1]
    rank = jnp.sum(onehot * csum, axis=1) - 1
    padded = (counts + MOE_ROWS - 1) // MOE_ROWS * MOE_ROWS
    pad_end = jnp.cumsum(padded)
    pad_start = pad_end - padded
    dest = pad_start[flat_exp] + rank
    slot_asg = jnp.full((cap,), -1, jnp.int32).at[dest].set(jnp.arange(n_assign, dtype=jnp.int32))
    valid = slot_asg >= 0
    slot_tok = jnp.where(valid, slot_asg // TOP_K, 0)
    spare = n_assign + jnp.cumsum((~valid).astype(jnp.int32)) - 1
    slot_dst = jnp.where(valid, (slot_asg % TOP_K) * n_tok + slot_asg // TOP_K, spare)
    block_exp = jnp.minimum(jnp.searchsorted(pad_end, jnp.arange(n_blocks, dtype=jnp.int32) * MOE_ROWS, side='right'),
                            N_EXPERTS - 1).astype(jnp.int32)
    return (slot_tok.reshape(n_blocks, 1, MOE_ROWS), slot_dst.reshape(n_blocks, 1, MOE_ROWS), block_exp)


def _expert_body(bexp_ref, tok0_ref, tok_next_ref, dst_prev_ref, dst_cur_ref, x_hbm, g_ref, wg_ref, wu_ref, wd_ref,
                 y_hbm, xbuf, obuf, act_scr, sem_in, sem_out):
    del bexp_ref
    i = pl.program_id(0)
    last = pl.num_programs(0) - 1
    cur = i % 2
    nxt = 1 - cur
    ocur = i % OUT_SLOTS
    oprev = (i + OUT_SLOTS - 1) % OUT_SLOTS
    n_chunks = D_FF // FF_CHUNK

    def gather_row(tok_ref, r, slot):
        return pltpu.make_async_copy(x_hbm.at[pl.ds(tok_ref[0, 0, r], 1)], xbuf.at[slot, pl.ds(r, 1)], sem_in.at[slot])

    def scatter_row(dst_ref, r, slot):
        return pltpu.make_async_copy(obuf.at[slot, pl.ds(r, 1)], y_hbm.at[pl.ds(dst_ref[0, 0, r], 1)], sem_out.at[slot])

    def wait_gather(slot):
        pltpu.make_async_copy(x_hbm.at[pl.ds(0, MOE_ROWS)], xbuf.at[slot], sem_in.at[slot]).wait()

    def wait_scatter(slot):
        pltpu.make_async_copy(obuf.at[slot], y_hbm.at[pl.ds(0, MOE_ROWS)], sem_out.at[slot]).wait()

    @pl.when(i == 0)
    def _():
        def start(r, _):
            gather_row(tok0_ref, r, 0).start()
            return 0
        lax.fori_loop(0, MOE_ROWS, start, 0)
        obuf[OUT_SLOTS - 1] = jnp.zeros((MOE_ROWS, D_MODEL), F32)

    wait_gather(cur)

    @pl.when(i >= OUT_SLOTS - 1)
    def _():
        wait_scatter(ocur)

    h = _rms(xbuf[cur], g_ref[...]).astype(BF16)
    out_chunks = D_MODEL // FF_CHUNK
    stages = n_chunks + out_chunks
    gather_stages = stages // 2
    rows_per_gather = -(-MOE_ROWS // gather_stages)
    rows_per_scatter = -(-MOE_ROWS // (stages - gather_stages))

    def start_rows(stage):
        if stage < gather_stages:
            for r in range(stage * rows_per_gather, min((stage + 1) * rows_per_gather, MOE_ROWS)):
                gather_row(tok_next_ref, r, nxt).start()
        else:
            st = stage - gather_stages
            for r in range(st * rows_per_scatter, min((st + 1) * rows_per_scatter, MOE_ROWS)):
                scatter_row(dst_prev_ref, r, oprev).start()

    def zero_after_starts():
        probe = pltpu.bitcast(xbuf[cur, 0:8, 0:FF_CHUNK], jnp.uint32)
        return pltpu.bitcast(lax.shift_right_logical(probe, jnp.uint32(32)), F32)[0:1, :]

    for j in range(n_chunks):
        start_rows(j)
        cols = slice(j * FF_CHUNK, (j + 1) * FF_CHUNK)
        gate = jnp.dot(h, wg_ref[0, :, cols], preferred_element_type=F32)
        up = jnp.dot(h, wu_ref[0, :, cols], preferred_element_type=F32)
        if j < gather_stages:
            up = up + zero_after_starts()
        act_scr[:, cols] = (gate * jax.nn.sigmoid(gate) * up).astype(BF16)

    act = act_scr[...]
    for n in range(out_chunks):
        start_rows(n_chunks + n)
        cols = slice(n * FF_CHUNK, (n + 1) * FF_CHUNK)
        obuf[ocur, :, cols] = jnp.dot(act, wd_ref[0, :, cols], preferred_element_type=F32)

    @pl.when(i == last)
    def _():
        def start(r, _):
            scatter_row(dst_cur_ref, r, ocur).start()
            return 0
        lax.fori_loop(0, MOE_ROWS, start, 0)
        for slot in range(OUT_SLOTS):
            wait_scatter(slot)
        wait_gather(nxt)


def moe_experts(x2, gain, slot_tok, slot_dst, block_exp, wg, wu, wd):
    n_blocks = block_exp.shape[0]
    cap = n_blocks * MOE_ROWS
    spare = (cap + jnp.arange(MOE_ROWS, dtype=jnp.int32)).reshape(1, 1, MOE_ROWS)
    dst_ext = jnp.concatenate([spare, slot_dst], axis=0)
    smem = lambda index: pl.BlockSpec((1, 1, MOE_ROWS), index, memory_space=pltpu.SMEM)
    grid_spec = pltpu.PrefetchScalarGridSpec(
        num_scalar_prefetch=1,
        grid=(n_blocks,),
        in_specs=[smem(lambda i, be: (0, 0, 0)),
                  smem(lambda i, be: (jnp.minimum(i + 1, n_blocks - 1), 0, 0)),
                  smem(lambda i, be: (i, 0, 0)),
                  smem(lambda i, be: (i + 1, 0, 0)),
                  pl.BlockSpec(memory_space=pl.ANY),
                  pl.BlockSpec((1, D_MODEL), lambda i, be: (0, 0)),
                  pl.BlockSpec((1, D_MODEL, D_FF), lambda i, be: (be[i], 0, 0)),
                  pl.BlockSpec((1, D_MODEL, D_FF), lambda i, be: (be[i], 0, 0)),
                  pl.BlockSpec((1, D_FF, D_MODEL), lambda i, be: (be[i], 0, 0))],
        out_specs=pl.BlockSpec(memory_space=pl.ANY),
        scratch_shapes=[pltpu.VMEM((2, MOE_ROWS, D_MODEL), F32), pltpu.VMEM((OUT_SLOTS, MOE_ROWS, D_MODEL), F32),
                        pltpu.VMEM((MOE_ROWS, D_FF), BF16),
                        pltpu.SemaphoreType.DMA((2,)), pltpu.SemaphoreType.DMA((OUT_SLOTS,))],
    )
    return pl.pallas_call(
        _expert_body,
        grid_spec=grid_spec,
        out_shape=jax.ShapeDtypeStruct((cap + MOE_ROWS, D_MODEL), F32),
        compiler_params=_params(("arbitrary",)),
        name="moe_experts",
    )(block_exp, slot_tok, slot_tok, dst_ext, dst_ext, x2, gain.reshape(1, D_MODEL), wg, wu, wd)


def _combine_body(x_ref, y1_ref, y2_ref, r_ref, o_ref):
    route = r_ref[...]
    o_ref[...] = x_ref[...] + route[:, 2:3] * y1_ref[...] + route[:, 3:4] * y2_ref[...]


def moe_combine(x2, y, route, tm=512):
    n_tok = x2.shape[0]
    tiles = n_tok // tm
    return pl.pallas_call(
        _combine_body,
        grid=(tiles,),
        in_specs=[pl.BlockSpec((tm, D_MODEL), lambda i: (i, 0)),
                  pl.BlockSpec((tm, D_MODEL), lambda i: (i, 0)),
                  pl.BlockSpec((tm, D_MODEL), lambda i: (tiles + i, 0)),
                  pl.BlockSpec((tm, ROUTE_LANES), lambda i: (i, 0))],
        out_specs=pl.BlockSpec((tm, D_MODEL), lambda i: (i, 0)),
        out_shape=jax.ShapeDtypeStruct((n_tok, D_MODEL), F32),
        compiler_params=_params(("parallel",)),
        name="moe_combine",
    )(x2, y, y, route)


def moe_ffn(x2, gain, router_w, router_b, wg, wu, wd):
    n_tok = x2.shape[0]
    route = moe_router(x2, gain, router_w, router_b)
    experts = route[:, :TOP_K].astype(jnp.int32)
    slot_tok, slot_dst, block_exp = _moe_dispatch(experts, n_tok)
    y = moe_experts(x2, gain, slot_tok, slot_dst, block_exp, wg, wu, wd)
    return moe_combine(x2, y, route)


def kernel(x, ln_mix, w_in, w_out, ssd_conv_w, ssd_conv_b, ssd_dt_bias, ssd_a_log, ssd_d, ssd_norm, att_q_norm, att_k_norm, att_out_norm, rwkv_mu, rwkv_w0, rwkv_w2, rwkv_a0, rwkv_a2, rwkv_g2, rwkv_k_k, rwkv_k_a, rwkv_r_k, rwkv_ln_w, rwkv_ln_b, s5_a_re, s5_a_im, s5_b_re, s5_b_im, s5_c_re, s5_c_im, s5_log_dt, s5_d, s5_glu_w, s5_glu_b, s5_out_norm, ln_ffn, ffn_w_gate, ffn_w_up, ffn_w_down, moe_router_w, moe_router_b, moe_w_gate, moe_w_up, moe_w_down):
    bsz, seq, dm = x.shape
    n_tok = bsz * seq
    depth = ln_mix.shape[0]
    x2 = x.reshape(n_tok, dm)
    for layer in range(depth):
        w_seg = _segment_w_in(w_in[layer])
        p_ssd, p_att, p_rwkv, p_s5 = inproj(x2, ln_mix[layer], w_seg, bsz, seq)
        y_a = ssd_mixer(p_ssd, ssd_conv_w[layer], ssd_conv_b[layer], ssd_dt_bias[layer], ssd_a_log[layer],
                        ssd_d[layer], ssd_norm[layer], bsz, seq)
        y_b = attention_mixer(p_att, att_q_norm[layer], att_k_norm[layer], att_out_norm[layer], bsz, seq)
        y_c = rwkv_mixer(p_rwkv, rwkv_mu[layer], rwkv_w0[layer], rwkv_w2[layer], rwkv_a0[layer], rwkv_a2[layer],
                         rwkv_g2[layer], rwkv_k_k[layer], rwkv_k_a[layer], rwkv_r_k[layer], rwkv_ln_w[layer],
                         rwkv_ln_b[layer], bsz, seq)
        y_d = s5_mixer(p_s5, s5_a_re[layer], s5_a_im[layer], s5_b_re[layer], s5_b_im[layer], s5_c_re[layer],
                       s5_c_im[layer], s5_log_dt[layer], s5_d[layer], s5_glu_w[layer], s5_glu_b[layer],
                       s5_out_norm[layer], bsz, seq)
        y_d_tm = y_d.reshape(seq, bsz * GROUP_WIDTH)
        x2 = outproj(x2, y_a, y_b, y_c, y_d_tm, w_out[layer].astype(BF16), bsz, seq)
        idx = layer // 2
        if layer % 2 == 0:
            x2 = dense_ffn(x2, ln_ffn[layer], ffn_w_gate[idx].astype(BF16), ffn_w_up[idx].astype(BF16),
                           ffn_w_down[idx].astype(BF16))
        else:
            x2 = moe_ffn(x2, ln_ffn[layer], moe_router_w[idx], moe_router_b[idx], moe_w_gate[idx].astype(BF16),
                         moe_w_up[idx].astype(BF16), moe_w_down[idx].astype(BF16))
    return x2.reshape(bsz, seq, dm)
```

```python
import functools
import math

import jax
import jax.numpy as jnp
from jax import lax
from jax.experimental import pallas as pl
from jax.experimental.pallas import tpu as pltpu

F32 = jnp.float32
BF16 = jnp.bfloat16

D_MODEL = 1024
GROUP_WIDTH = 256
HEAD_DIM = 64
NORM_EPS = 1e-6
D_FF = 2816

SSD_XBC = 512
SSD_SEG = 1024
ATT_SEG = 768
RWKV_SEG = 1024
S5_SEG = 256
SEG_WIDTHS = (SSD_SEG, ATT_SEG, RWKV_SEG, S5_SEG)
SEG_TOTAL = sum(SEG_WIDTHS)

VMEM_LIMIT = 56 * 1024 * 1024


def _params(sem):
    return pltpu.CompilerParams(dimension_semantics=sem, vmem_limit_bytes=VMEM_LIMIT)


def _rms(x, gain):
    ms = jnp.mean(x * x, axis=-1, keepdims=True)
    return x * lax.rsqrt(ms + NORM_EPS) * gain


def _inproj_body(x_ref, g_ref, w_ref, o_ssd, o_att, o_rwkv, o_s5):
    h = _rms(x_ref[...], g_ref[...]).astype(BF16)
    off = 0
    for o_ref in (o_ssd, o_att, o_rwkv, o_s5):
        n = o_ref.shape[-1]
        o_ref[...] = jnp.dot(h, w_ref[:, off:off + n], preferred_element_type=F32)
        off += n


def inproj(x2, gain, w_seg, bsz, seq, tm=512):
    n_tok = x2.shape[0]
    tiles_per_seq = seq // tm
    row = lambda i: (i, 0)
    return pl.pallas_call(
        _inproj_body,
        grid=(n_tok // tm,),
        in_specs=[pl.BlockSpec((tm, D_MODEL), row),
                  pl.BlockSpec((1, D_MODEL), lambda i: (0, 0)),
                  pl.BlockSpec((D_MODEL, SEG_TOTAL), lambda i: (0, 0))],
        out_specs=[pl.BlockSpec((tm, SSD_SEG), row),
                   pl.BlockSpec((tm, ATT_SEG), row),
                   pl.BlockSpec((tm, RWKV_SEG), row),
                   pl.BlockSpec((tm, S5_SEG), lambda i: (i % tiles_per_seq, i // tiles_per_seq))],
        out_shape=[jax.ShapeDtypeStruct((n_tok, SSD_SEG), F32),
                   jax.ShapeDtypeStruct((n_tok, ATT_SEG), F32),
                   jax.ShapeDtypeStruct((n_tok, RWKV_SEG), F32),
                   jax.ShapeDtypeStruct((seq, bsz * S5_SEG), F32)],
        compiler_params=_params(("parallel",)),
        name="inproj",
    )(x2, gain.reshape(1, D_MODEL), w_seg)


def _segment_w_in(w):
    dt_cols = jnp.repeat(w[:, 768:772], HEAD_DIM, axis=1)
    return jnp.concatenate([w[:, :768], dt_cols, w[:, 772:]], axis=1).astype(BF16)


def _mix_out(x_ref, y_refs, wo_ref):
    acc = x_ref[...]
    for j, y_ref in enumerate(y_refs):
        acc = acc + jnp.dot(y_ref[...].astype(BF16), wo_ref[j * GROUP_WIDTH:(j + 1) * GROUP_WIDTH, :],
                            preferred_element_type=F32)
    return acc


def _mix_out_specs(tm, seq):
    tiles_per_seq = seq // tm
    row = lambda i: (i, 0)
    return [pl.BlockSpec((tm, D_MODEL), row),
            pl.BlockSpec((tm, GROUP_WIDTH), row),
            pl.BlockSpec((tm, GROUP_WIDTH), row),
            pl.BlockSpec((tm, GROUP_WIDTH), row),
            pl.BlockSpec((tm, GROUP_WIDTH), lambda i: (i % tiles_per_seq, i // tiles_per_seq))]


def _resident(shape):
    return pl.BlockSpec(shape, lambda i: (0,) * len(shape), pipeline_mode=pl.Buffered(1))


FFN_COLS = 256


def _outproj_ffn_body(x_ref, ya_ref, yb_ref, yc_ref, yd_ref, wo_ref, g_ref, wg_ref, wu_ref, wd_ref, o_ref,
                      xn_scr, act_scr):
    xn = _mix_out(x_ref, (ya_ref, yb_ref, yc_ref, yd_ref), wo_ref)
    xn_scr[...] = xn
    h = _rms(xn, g_ref[...]).astype(BF16)
    for j in range(D_FF // FFN_COLS):
        cols = slice(j * FFN_COLS, (j + 1) * FFN_COLS)
        gate = jnp.dot(h, wg_ref[:, cols], preferred_element_type=F32)
        up = jnp.dot(h, wu_ref[:, cols], preferred_element_type=F32)
        act_scr[:, cols] = (gate * jax.nn.sigmoid(gate) * up).astype(BF16)
    act = act_scr[...]
    for n in range(D_MODEL // FFN_COLS):
        cols = slice(n * FFN_COLS, (n + 1) * FFN_COLS)
        o_ref[:, cols] = xn_scr[:, cols] + jnp.dot(act, wd_ref[:, cols], preferred_element_type=F32)


def outproj_ffn(x2, ya, yb, yc, yd_tm, w_out_bf, gain, wg, wu, wd, seq, tm=512):
    n_tok = x2.shape[0]
    return pl.pallas_call(
        _outproj_ffn_body,
        grid=(n_tok // tm,),
        in_specs=_mix_out_specs(tm, seq) + [
            _resident((D_MODEL, D_MODEL)), pl.BlockSpec((1, D_MODEL), lambda i: (0, 0)),
            _resident((D_MODEL, D_FF)), _resident((D_MODEL, D_FF)), _resident((D_FF, D_MODEL))],
        out_specs=pl.BlockSpec((tm, D_MODEL), lambda i: (i, 0)),
        out_shape=jax.ShapeDtypeStruct((n_tok, D_MODEL), F32),
        scratch_shapes=[pltpu.VMEM((tm, D_MODEL), F32), pltpu.VMEM((tm, D_FF), BF16)],
        compiler_params=_params(("parallel",)),
        name="outproj_ffn",
    )(x2, ya, yb, yc, yd_tm, w_out_bf, gain.reshape(1, D_MODEL), wg, wu, wd)


S5_GROUPS = 16
S5_GROUP = 16
S5_STATE = 64
S5_CH = S5_GROUPS * S5_STATE
S5_BT = 8
S5_TIME_SEG = 32


def _s5_body(u_ref, wb_ref, wc_ref, are_ref, aim_ref, d_ref, gw_ref, gb_ref, gain_ref, o_ref, h_scr, st_scr):
    tt = u_ref.shape[0]

    @pl.when(pl.program_id(1) == 0)
    def _():
        st_scr[...] = jnp.zeros_like(st_scr)

    a_re = jnp.broadcast_to(are_ref[...], (S5_BT, S5_CH))
    a_im = jnp.broadcast_to(aim_ref[...], (S5_BT, S5_CH))
    seg = S5_TIME_SEG
    n_seg = tt // seg

    def inputs(q):
        return u_ref[q * seg:(q + 1) * seg].reshape(seg * S5_BT, GROUP_WIDTH)

    def project_in(q):
        bu = jnp.dot(inputs(q).astype(BF16), wb_ref[...], preferred_element_type=F32)
        h_scr[q * seg:(q + 1) * seg] = bu.reshape(seg, S5_BT, 2 * S5_CH)

    def scan(q, h_re, h_im):
        for t in range(q * seg, (q + 1) * seg):
            n_re = a_re * h_re - a_im * h_im + h_scr[t, :, :S5_CH]
            n_im = a_re * h_im + a_im * h_re + h_scr[t, :, S5_CH:]
            h_scr[t, :, :S5_CH] = n_re
            h_scr[t, :, S5_CH:] = n_im
            h_re, h_im = n_re, n_im
        return h_re, h_im

    def project_out(q):
        hall = h_scr[q * seg:(q + 1) * seg].reshape(seg * S5_BT, 2 * S5_CH)
        y = jnp.dot(hall.astype(BF16), wc_ref[...], preferred_element_type=F32) + d_ref[...] * inputs(q)
        hh = jax.nn.gelu(y)
        gl = jnp.dot(hh.astype(BF16), gw_ref[...], preferred_element_type=F32) + gb_ref[...]
        out = _rms(hh * jax.nn.sigmoid(gl), gain_ref[...])
        o_ref[q * seg:(q + 1) * seg] = out.reshape(seg, S5_BT, GROUP_WIDTH)

    h_re, h_im = st_scr[0], st_scr[1]
    project_in(0)
    for q in range(n_seg):
        if q + 1 < n_seg:
            project_in(q + 1)
        h_re, h_im = scan(q, h_re, h_im)
        project_out(q)
    st_scr[0] = h_re
    st_scr[1] = h_im


def _s5_weights(a_re, a_im, b_re, b_im, c_re, c_im, log_dt):
    dt = jnp.exp(log_dt)[:, None]
    mag = jnp.exp(dt * a_re)
    ab_re = mag * jnp.cos(dt * a_im)
    ab_im = mag * jnp.sin(dt * a_im)
    den = a_re * a_re + a_im * a_im
    f_re = ((ab_re - 1.0) * a_re + ab_im * a_im) / den
    f_im = (ab_im * a_re - (ab_re - 1.0) * a_im) / den
    bb_re = f_re[..., None] * b_re - f_im[..., None] * b_im
    bb_im = f_re[..., None] * b_im + f_im[..., None] * b_re
    eye = jnp.eye(S5_GROUPS, dtype=F32)
    expand_b = lambda m: jnp.einsum('gpc,gh->gchp', m, eye).reshape(GROUP_WIDTH, S5_CH)
    expand_c = lambda m: jnp.einsum('gcp,gh->gphc', m, eye).reshape(S5_CH, GROUP_WIDTH)
    wb = jnp.concatenate([expand_b(bb_re), expand_b(bb_im)], axis=1).astype(BF16)
    wc = jnp.concatenate([expand_c(c_re), -expand_c(c_im)], axis=0).astype(BF16)
    return wb, wc, ab_re.reshape(1, S5_CH), ab_im.reshape(1, S5_CH)


def s5_mixer(u_tm, a_re, a_im, b_re, b_im, c_re, c_im, log_dt, d_skip, glu_w, glu_b, out_gain, bsz, seq, tt=128):
    wb, wc, are, aim = _s5_weights(a_re, a_im, b_re, b_im, c_re, c_im, log_dt)
    u3 = u_tm.reshape(seq, bsz, GROUP_WIDTH)
    const = lambda shape: pl.BlockSpec(shape, lambda i, j: (0,) * len(shape))
    row = lambda v: v.reshape(1, GROUP_WIDTH)
    return pl.pallas_call(
        _s5_body,
        grid=(bsz // S5_BT, seq // tt),
        in_specs=[pl.BlockSpec((tt, S5_BT, GROUP_WIDTH), lambda i, j: (j, i, 0)),
                  const((GROUP_WIDTH, 2 * S5_CH)), const((2 * S5_CH, GROUP_WIDTH)),
                  const((1, S5_CH)), const((1, S5_CH)), const((1, GROUP_WIDTH)),
                  const((GROUP_WIDTH, GROUP_WIDTH)), const((1, GROUP_WIDTH)), const((1, GROUP_WIDTH))],
        out_specs=pl.BlockSpec((tt, S5_BT, GROUP_WIDTH), lambda i, j: (j, i, 0)),
        out_shape=jax.ShapeDtypeStruct((seq, bsz, GROUP_WIDTH), F32),
        scratch_shapes=[pltpu.VMEM((tt, S5_BT, 2 * S5_CH), F32), pltpu.VMEM((2, S5_BT, S5_CH), F32)],
        compiler_params=_params(("parallel", "arbitrary")),
        name="s5_mixer",
    )(u3, wb, wc, are, aim, row(d_skip), glu_w.astype(BF16), row(glu_b), row(out_gain))


SSD_CHUNK = 128
SSD_CONV = 4
SSD_HEADS = 4
SSD_TAIL = 8
HIGHEST = lax.Precision.HIGHEST


def _dot_nt(a, b, **kw):
    return lax.dot_general(a, b, (((1,), (1,)), ((), ())), preferred_element_type=F32, **kw)


def _dot_tn(a, b, **kw):
    return lax.dot_general(a, b, (((0,), (0,)), ((), ())), preferred_element_type=F32, **kw)


def _ssd_body(p_ref, cw_ref, cb_ref, dtb_ref, arow_ref, drow_ref, gain_ref, o_ref, tail_scr, st_scr):
    tb = p_ref.shape[0]
    L = SSD_CHUNK

    @pl.when(pl.program_id(1) == 0)
    def _():
        tail_scr[...] = jnp.zeros_like(tail_scr)
        st_scr[...] = jnp.zeros_like(st_scr)

    raw = p_ref[:, :SSD_XBC]
    ext = jnp.concatenate([tail_scr[...], raw], axis=0)
    conv = cb_ref[...] + cw_ref[SSD_CONV - 1:SSD_CONV, :] * raw
    for j in range(SSD_CONV - 1):
        shifted = pltpu.roll(ext, SSD_CONV - 1 - j, axis=0)[SSD_TAIL:, :]
        conv = conv + cw_ref[j:j + 1, :] * shifted
    tail_scr[...] = raw[tb - SSD_TAIL:, :]
    xc = conv * jax.nn.sigmoid(conv)

    dt = jax.nn.softplus(p_ref[:, 768:1024] + dtb_ref[...])
    a = dt * arow_ref[...]

    ii = lax.broadcasted_iota(jnp.int32, (L, L), 0)
    jj = lax.broadcasted_iota(jnp.int32, (L, L), 1)
    causal = ii >= jj
    tril = causal.astype(BF16)
    low_half = jj < HEAD_DIM
    lane2 = lax.broadcasted_iota(jnp.int32, (L, GROUP_WIDTH), 1)
    lane1 = lax.broadcasted_iota(jnp.int32, (L, L), 1)
    row_grp = lax.broadcasted_iota(jnp.int32, (L, GROUP_WIDTH), 0) // HEAD_DIM
    state_mask = row_grp == lane2 // (2 * HEAD_DIM)

    for c in range(tb // L):
        sl = slice(c * L, (c + 1) * L)
        xs = xc[sl, :GROUP_WIDTH]
        bm = xc[sl, GROUP_WIDTH:GROUP_WIDTH + L]
        cm = xc[sl, GROUP_WIDTH + L:]
        xdt = xs * dt[sl]
        a_hi = a[sl].astype(BF16)
        a_lo = (a[sl] - a_hi.astype(F32)).astype(BF16)
        cs = (jnp.dot(tril, a_hi, preferred_element_type=F32)
              + jnp.dot(tril, a_lo, preferred_element_type=F32))
        cs_end = cs[L - 1:L, :]
        st = st_scr[...]
        y = jnp.exp(cs) * jnp.dot(cm.astype(BF16), st.astype(BF16), preferred_element_type=F32)
        for g in range(2):
            cg = jnp.where((lane1 // HEAD_DIM) == g, cm, 0.0)
            gram = _dot_nt(cg.astype(BF16), bm.astype(BF16))
            pair = cs[:, g * L:(g + 1) * L]
            swapped = pltpu.roll(pair, HEAD_DIM, axis=1)
            for k in range(2):
                h = 2 * g + k
                col = jnp.where(low_half, pair, swapped) if k == 0 else jnp.where(low_half, swapped, pair)
                seg = col - col.T
                decay = jnp.where(causal, jnp.exp(jnp.minimum(seg, 0.0)), 0.0)
                yd = jnp.dot((gram * decay).astype(BF16), xdt.astype(BF16), preferred_element_type=F32)
                y = y + jnp.where((lane2 // HEAD_DIM) == h, yd, 0.0)
        upd = _dot_tn(bm.astype(BF16), (xdt * jnp.exp(cs_end - cs)).astype(BF16))
        st_scr[...] = st * jnp.exp(cs_end) + jnp.where(state_mask, upd, 0.0)
        y = y + drow_ref[...] * xs
        z = p_ref[sl, 512:768]
        o_ref[sl, :] = _rms(y * (z * jax.nn.sigmoid(z)), gain_ref[...])


def ssd_mixer(p_ssd, conv_w, conv_b, dt_bias, a_log, d_skip, norm_gain, bsz, seq, tb=512):
    n_tok = p_ssd.shape[0]
    blocks = seq // tb
    per_head = lambda v: jnp.repeat(v.astype(F32), HEAD_DIM).reshape(1, GROUP_WIDTH)
    const = lambda shape: pl.BlockSpec(shape, lambda b, j: (0,) * len(shape))
    return pl.pallas_call(
        _ssd_body,
        grid=(bsz, blocks),
        in_specs=[pl.BlockSpec((tb, SSD_SEG), lambda b, j: (b * blocks + j, 0)),
                  const((SSD_CONV, SSD_XBC)), const((1, SSD_XBC)), const((1, GROUP_WIDTH)),
                  const((1, GROUP_WIDTH)), const((1, GROUP_WIDTH)), const((1, GROUP_WIDTH))],
        out_specs=pl.BlockSpec((tb, GROUP_WIDTH), lambda b, j: (b * blocks + j, 0)),
        out_shape=jax.ShapeDtypeStruct((n_tok, GROUP_WIDTH), F32),
        scratch_shapes=[pltpu.VMEM((SSD_TAIL, SSD_XBC), F32), pltpu.VMEM((2 * HEAD_DIM, GROUP_WIDTH), F32)],
        compiler_params=_params(("parallel", "arbitrary")),
        name="ssd_mixer",
    )(p_ssd, conv_w, conv_b.reshape(1, SSD_XBC), per_head(dt_bias), per_head(-jnp.exp(a_log)), per_head(d_skip),
      norm_gain.reshape(1, GROUP_WIDTH))


ATT_HEADS = 4
ATT_BLOCK = 128
DILATED_PATTERNS = ((128, 1), (512, 4), (2048, 16))
MASKED = -1e30
ATT_PAR = 4


def _att_bias():
    slopes = [2.0 ** (-8.0 / ATT_HEADS * (h + 1)) for h in range(ATT_HEADS)]
    q_idx = jnp.arange(ATT_BLOCK)[:, None]
    k_idx = jnp.arange(2 * ATT_BLOCK)[None, :] - ATT_BLOCK
    rel = q_idx - k_idx
    out = []
    for window, dilation in DILATED_PATTERNS:
        span = window // dilation
        valid = (rel >= 0) & (rel <= span)
        per_head = [jnp.where(valid, -s * (rel * dilation).astype(F32), MASKED) for s in slopes]
        out.append(jnp.stack(per_head))
    return jnp.stack(out)


def _ld2(scr, rows):
    return jnp.concatenate([scr[0, rows, :], scr[1, rows, :]], axis=1)


def _st2(scr, rows, val):
    scr[0, rows, :] = val[:, :ATT_BLOCK]
    scr[1, rows, :] = val[:, ATT_BLOCK:]


def _att_body(qkv_ref, gmat_ref, qg_ref, kg_ref, og_ref, bias_ref, o_ref, q_scr, k_scr, v_scr, u_scr, m_scr, s_scr):
    seq = qkv_ref.shape[0]
    blk = ATT_BLOCK
    lane = lax.broadcasted_iota(jnp.int32, (blk, GROUP_WIDTH), 1) // HEAD_DIM
    every = pl.ds(0, seq)

    def head_norm(t, gain):
        sq = t * t
        hi = sq.astype(BF16)
        lo = (sq - hi.astype(F32)).astype(BF16)
        ms = (jnp.dot(hi, gmat_ref[...], preferred_element_type=F32)
              + jnp.dot(lo, gmat_ref[...], preferred_element_type=F32))
        return t * lax.rsqrt(ms + NORM_EPS) * gain

    _st2(q_scr, every, head_norm(qkv_ref[:, :GROUP_WIDTH], qg_ref[...]) * (1.0 / math.sqrt(HEAD_DIM)))
    _st2(k_scr, every, head_norm(qkv_ref[:, GROUP_WIDTH:2 * GROUP_WIDTH], kg_ref[...]))
    _st2(v_scr, every, qkv_ref[:, 2 * GROUP_WIDTH:])

    prev_cols = lax.broadcasted_iota(jnp.int32, (1, 2 * blk), 1) < blk
    heads = range(ATT_HEADS)

    def blocks(pat, specs):
        pairs = [(b, h) for b in range(len(specs)) for h in heads]
        q, keys, vals, gate = [], [], [], []
        for rows, prev_rows, has_prev in specs:
            q.append(_ld2(q_scr, rows))
            if prev_rows is None:
                keys.append(_ld2(k_scr, rows).astype(BF16))
                vals.append(_ld2(v_scr, rows).astype(BF16))
                gate.append(None)
            else:
                keys.append(jnp.concatenate([_ld2(k_scr, prev_rows), _ld2(k_scr, rows)], axis=0).astype(BF16))
                vals.append(jnp.concatenate([_ld2(v_scr, prev_rows), _ld2(v_scr, rows)], axis=0).astype(BF16))
                gate.append(jnp.where(jnp.logical_or(has_prev, jnp.logical_not(prev_cols)), 0.0, MASKED))
        bias = {(b, h): (bias_ref[pat, h, :, blk:] if gate[b] is None else bias_ref[pat, h] + gate[b])
                for b, h in pairs}
        qh = {(b, h): jnp.where(lane == h, q[b], 0.0).astype(BF16) for b, h in pairs}
        sc = {(b, h): _dot_nt(qh[b, h], keys[b]) + bias[b, h] for b, h in pairs}
        m = {bh: jnp.max(sc[bh], axis=-1, keepdims=True) for bh in pairs}
        p = {bh: jnp.exp(sc[bh] - m[bh]) for bh in pairs}
        den = {bh: jnp.sum(p[bh], axis=-1, keepdims=True) for bh in pairs}
        u = {(b, h): jnp.dot(p[b, h].astype(BF16), vals[b], preferred_element_type=F32) for b, h in pairs}
        out = []
        for b in range(len(specs)):
            u_all, m_all, s_all = u[b, 0], m[b, 0], den[b, 0]
            for h in heads[1:]:
                sel = lane == h
                u_all = jnp.where(sel, u[b, h], u_all)
                m_all = jnp.where(sel, m[b, h], m_all)
                s_all = jnp.where(sel, den[b, h], s_all)
            out.append((u_all, m_all, s_all))
        return out

    def merge(rows, u, m, s):
        m0 = _ld2(m_scr, rows)
        m_new = jnp.maximum(m0, m)
        alpha = jnp.exp(m0 - m_new)
        beta = jnp.exp(m - m_new)
        return alpha * _ld2(u_scr, rows) + beta * u, m_new, alpha * _ld2(s_scr, rows) + beta * s

    def save(rows, u, m, s):
        _st2(u_scr, rows, u)
        _st2(m_scr, rows, m)
        _st2(s_scr, rows, s)

    par = ATT_PAR
    n_iter = 16 // par

    def body16(i, _):
        specs = [(pl.ds(i + n_iter * b, blk, stride=16), None, None) for b in range(par)]
        for spec, res in zip(specs, blocks(2, specs)):
            save(spec[0], *res)
        return 0
    lax.fori_loop(0, n_iter, body16, 0)

    def body4(i, _):
        specs = []
        for b in range(par):
            idx = i + n_iter * b
            r = idx % 4
            n = idx // 4
            specs.append((pl.ds(n * (4 * blk) + r, blk, stride=4),
                          pl.ds(jnp.maximum(n - 1, 0) * (4 * blk) + r, blk, stride=4), n > 0))
        for spec, res in zip(specs, blocks(1, specs)):
            save(spec[0], *merge(spec[0], *res))
        return 0
    lax.fori_loop(0, n_iter, body4, 0)

    def body1(i, _):
        specs = []
        for b in range(par):
            n = i + (seq // blk // par) * b
            specs.append((pl.ds(pl.multiple_of(n * blk, blk), blk),
                          pl.ds(pl.multiple_of(jnp.maximum(n - 1, 0) * blk, blk), blk), n > 0))
        for spec, res in zip(specs, blocks(0, specs)):
            u, m, s = merge(spec[0], *res)
            o_ref[spec[0], :] = _rms(u / s, og_ref[...])
        return 0
    lax.fori_loop(0, seq // blk // par, body1, 0)


def attention_mixer(qkv, q_gain, k_gain, out_gain, bsz, seq):
    n_tok = qkv.shape[0]
    head_id = jnp.arange(GROUP_WIDTH) // HEAD_DIM
    gmat = ((head_id[:, None] == head_id[None, :]).astype(F32) / HEAD_DIM).astype(BF16)
    tile = lambda g: jnp.tile(g.astype(F32), ATT_HEADS).reshape(1, GROUP_WIDTH)
    const = lambda shape: pl.BlockSpec(shape, lambda b: (0,) * len(shape))
    return pl.pallas_call(
        _att_body,
        grid=(bsz,),
        in_specs=[pl.BlockSpec((seq, ATT_SEG), lambda b: (b, 0)),
                  const((GROUP_WIDTH, GROUP_WIDTH)), const((1, GROUP_WIDTH)), const((1, GROUP_WIDTH)),
                  const((1, GROUP_WIDTH)), const((3, ATT_HEADS, ATT_BLOCK, 2 * ATT_BLOCK))],
        out_specs=pl.BlockSpec((seq, GROUP_WIDTH), lambda b: (b, 0)),
        out_shape=jax.ShapeDtypeStruct((n_tok, GROUP_WIDTH), F32),
        scratch_shapes=[pltpu.VMEM((2, seq, ATT_BLOCK), F32) for _ in range(6)],
        compiler_params=_params(("parallel",)),
        name="attention_mixer",
    )(qkv, gmat, tile(q_gain), tile(k_gain), out_gain.reshape(1, GROUP_WIDTH), _att_bias())


RWKV_HEADS = 4
RWKV_CHUNK = 64
RWKV_DECAY_SCALE = 0.606531
RWKV_GN_EPS = 64e-5
RWKV_MAT = RWKV_HEADS * RWKV_CHUNK


def _per_head_blocks(x):
    lane_head = lax.broadcasted_iota(jnp.int32, x.shape, 1) // HEAD_DIM
    return jnp.concatenate([jnp.where(lane_head == h, x, 0.0) for h in range(RWKV_HEADS)], axis=0)


def _rwkv_body(p_ref, mu_ref, w0_ref, w2_ref, a0_ref, a2_ref, g2_ref, kk_ref, ka_ref, rk_ref, lnw_ref, lnb_ref,
               hsum_ref, o_ref, prev_scr, z_scr):
    nb, tb = p_ref.shape[0], p_ref.shape[1]
    L = RWKV_CHUNK

    def mb(a, b):
        return jnp.dot(a.astype(BF16), b.astype(BF16), preferred_element_type=F32)

    def split(x):
        hi = x.astype(BF16)
        return hi, (x - hi.astype(F32)).astype(BF16)

    def head_sum(x):
        return jnp.dot(x.astype(BF16), hsum_ref[...], preferred_element_type=F32)

    @pl.when(pl.program_id(1) == 0)
    def _():
        prev_scr[...] = jnp.zeros_like(prev_scr)
        z_scr[...] = jnp.zeros_like(z_scr)

    ii = lax.broadcasted_iota(jnp.int32, (L, RWKV_MAT), 0)
    jj = lax.broadcasted_iota(jnp.int32, (L, RWKV_MAT), 1) % L
    eye_all = (ii == jj).astype(F32)
    ti = lax.broadcasted_iota(jnp.int32, (L, L), 0)
    tj = lax.broadcasted_iota(jnp.int32, (L, L), 1)
    tril = (ti >= tj).astype(BF16)
    first_row = lax.broadcasted_iota(jnp.int32, (L, RWKV_SEG), 0) == 0
    zi = lax.broadcasted_iota(jnp.int32, (RWKV_MAT, RWKV_MAT), 0) // HEAD_DIM
    zj = lax.broadcasted_iota(jnp.int32, (RWKV_MAT, RWKV_MAT), 1) // HEAD_DIM
    same_head = zi == zj

    def each(fn, *lists):
        return [fn(*args) for args in zip(*lists)]

    def chunks(c, _):
        seqs = list(range(nb))
        rows = pl.ds(pl.multiple_of(c * L, L), L)
        pc = [p_ref[s, rows, :] for s in seqs]
        prev = [jnp.where(first_row, prev_scr[s], pltpu.roll(pc[s], 1, axis=0)) for s in seqs]
        for s in seqs:
            prev_scr[s] = pc[s][L - 1:L, :]
        pf = each(lambda x, xp: x + mu_ref[...] * (xp - x), pc, prev)
        r = [x[:, :GROUP_WIDTH] for x in pf]
        k = [x[:, GROUP_WIDTH:2 * GROUP_WIDTH] for x in pf]
        v = [x[:, 2 * GROUP_WIDTH:3 * GROUP_WIDTH] for x in pf]
        wa = [x[:, 3 * GROUP_WIDTH:3 * GROUP_WIDTH + 128] for x in pf]
        gd = [x[:, 3 * GROUP_WIDTH + 128:] for x in pf]
        log_w = each(lambda x: -RWKV_DECAY_SCALE * jax.nn.sigmoid(
            w0_ref[...] + jnp.dot(jnp.tanh(x).astype(BF16), w2_ref[...], preferred_element_type=F32)), wa)
        alr = each(lambda x: jax.nn.sigmoid(
            a0_ref[...] + jnp.dot(x.astype(BF16), a2_ref[...], preferred_element_type=F32)), wa)
        g = each(lambda x: jnp.dot(jax.nn.sigmoid(x).astype(BF16), g2_ref[...], preferred_element_type=F32), gd)
        kk = each(lambda x: x * kk_ref[...], k)
        kk_ss = each(lambda x: head_sum(x * x), kk)
        kk = each(lambda x, ss: x * lax.rsqrt(ss + 1e-12), kk, kk_ss)
        k2 = each(lambda x, a: x * (1.0 + (a - 1.0) * ka_ref[...]), k, alr)

        def cumsum(lw):
            hi, lo = split(lw)
            return (jnp.dot(tril, hi, preferred_element_type=F32) + jnp.dot(tril, lo, preferred_element_type=F32))
        cs = each(cumsum, log_w)
        gam = each(jnp.exp, cs)
        gam_inv = each(lambda x: jnp.exp(-x), cs)
        a_hat = each(lambda x, c_, lw: -x * jnp.exp(c_ - lw), kk, cs, log_w)
        b_hat = each(lambda x, a, gi: x * a * gi, kk, alr, gam_inv)
        k_hat = each(lambda x, gi: x * gi, k2, gam_inv)
        r_hat = each(lambda x, gm: x * gm, r, gam)

        gram = each(lambda a, rr, b, kh: _dot_nt(
            jnp.concatenate([a, rr], axis=0).astype(BF16),
            jnp.concatenate([_per_head_blocks(b), _per_head_blocks(kh)], axis=0).astype(BF16)),
            a_hat, r_hat, b_hat, k_hat)
        n_all = [jnp.where(ii > jj, x[:L, :RWKV_MAT], 0.0) for x in gram]
        m_all = [jnp.where(ii > jj, x[:L, RWKV_MAT:], 0.0) for x in gram]
        gb_all = [jnp.where(ii >= jj, x[L:, :RWKV_MAT], 0.0) for x in gram]
        gk_all = [jnp.where(ii >= jj, x[L:, RWKV_MAT:], 0.0) for x in gram]

        w_all = [eye_all + n for n in n_all]
        n_pow = each(lambda n: mb(n, _per_head_blocks(n)), n_all)
        for _ in range(4):
            res = each(lambda n, w: mb(jnp.concatenate([n, w], axis=0), _per_head_blocks(n)), n_pow, w_all)
            n_pow = [x[:L] for x in res]
            w_all = each(lambda w, x: w + x[L:], w_all, res)
        w_all = each(lambda w, n: w + mb(w, _per_head_blocks(n)), w_all, n_pow)

        t1 = each(lambda m, vv: mb(m, _per_head_blocks(vv)), m_all, v)
        at_vt = each(lambda w, a, t: mb(w, jnp.concatenate([_per_head_blocks(a), _per_head_blocks(t)], axis=1)),
                     w_all, a_hat, t1)
        a_til = [x[:, :RWKV_MAT] for x in at_vt]
        v_til = [x[:, RWKV_MAT:] for x in at_vt]

        z = [z_scr[s] for s in seqs]
        ax_rx = each(lambda a, rr, zz: _dot_nt(jnp.concatenate([a, rr], axis=0).astype(BF16), zz.astype(BF16)),
                     a_til, r_hat, z)
        u = each(lambda x, vt: x[:L] + vt, ax_rx, v_til)
        y = each(lambda x, gb, gk, uu, vv: x[L:] + mb(
            jnp.concatenate([gb, gk], axis=1),
            jnp.concatenate([_per_head_blocks(uu), _per_head_blocks(vv)], axis=0)), ax_rx, gb_all, gk_all, u, v)
        d = each(lambda uu, vv, b, kh: _dot_tn(jnp.concatenate([uu, vv], axis=0).astype(BF16),
                                               jnp.concatenate([b, kh], axis=0).astype(BF16)), u, v, b_hat, k_hat)
        for s in seqs:
            z_scr[s] = (z[s] + jnp.where(same_head, d[s], 0.0)) * gam[s][L - 1:L, :]

        mean = each(lambda x: head_sum(x) * (1.0 / HEAD_DIM), y)
        yc = each(lambda x, m: x - m, y, mean)
        var = each(lambda x: head_sum(x * x) * (1.0 / HEAD_DIM), yc)
        yn = each(lambda x, vr: x * lax.rsqrt(vr + RWKV_GN_EPS) * lnw_ref[...] + lnb_ref[...], yc, var)
        bonus = each(lambda rr, kx, vv: head_sum(rr * kx * rk_ref[...]) * vv, r, k2, v)
        for s in seqs:
            o_ref[s, rows, :] = (yn[s] + bonus[s]) * g[s]
        return 0

    lax.fori_loop(0, tb // L, chunks, 0)


def rwkv_mixer(p_rwkv, mu, w0, w2, a0, a2, g2, k_k, k_a, r_k, ln_w, ln_b, bsz, seq, tb=256, nb=8):
    n_tok = p_rwkv.shape[0]
    nb = nb if bsz % nb == 0 else 1
    row = lambda t: t.reshape(1, -1).astype(F32)
    w2p = jnp.concatenate([w2, jnp.zeros_like(a2)], axis=0).astype(BF16)
    a2p = jnp.concatenate([jnp.zeros_like(w2), a2], axis=0).astype(BF16)
    head_id = jnp.arange(GROUP_WIDTH) // HEAD_DIM
    hsum = (head_id[:, None] == head_id[None, :]).astype(BF16)
    const = lambda shape: pl.BlockSpec(shape, lambda b, j: (0,) * len(shape))
    vec = const((1, GROUP_WIDTH))
    return pl.pallas_call(
        _rwkv_body,
        grid=(bsz // nb, seq // tb),
        in_specs=[pl.BlockSpec((nb, tb, RWKV_SEG), lambda b, j: (b, j, 0)),
                  const((1, RWKV_SEG)), vec, const((128, GROUP_WIDTH)), vec, const((128, GROUP_WIDTH)),
                  const((128, GROUP_WIDTH)), vec, vec, vec, vec, vec, const((GROUP_WIDTH, GROUP_WIDTH))],
        out_specs=pl.BlockSpec((nb, tb, GROUP_WIDTH), lambda b, j: (b, j, 0)),
        out_shape=jax.ShapeDtypeStruct((bsz, seq, GROUP_WIDTH), F32),
        scratch_shapes=[pltpu.VMEM((nb, 1, RWKV_SEG), F32), pltpu.VMEM((nb, RWKV_MAT, RWKV_MAT), F32)],
        compiler_params=_params(("parallel", "arbitrary")),
        name="rwkv_mixer",
    )(p_rwkv.reshape(bsz, seq, RWKV_SEG), row(mu), row(w0), w2p, row(a0), a2p, g2.astype(BF16), row(k_k), row(k_a),
      row(r_k), row(ln_w), row(ln_b), hsum).reshape(n_tok, GROUP_WIDTH)


N_EXPERTS = 8
TOP_K = 2
MOE_ROWS = 256
ROUTE_LANES = 128
FF_CHUNK = 256
OUT_SLOTS = 3


def _outproj_router_body(x_ref, ya_ref, yb_ref, yc_ref, yd_ref, wo_ref, g_ref, w_ref, b_ref, xn_ref, o_ref):
    xn = _mix_out(x_ref, (ya_ref, yb_ref, yc_ref, yd_ref), wo_ref)
    xn_ref[...] = xn
    h = _rms(xn, g_ref[...])
    h_hi = h.astype(BF16)
    h_lo = (h - h_hi.astype(F32)).astype(BF16)
    both = jnp.dot(h_hi, w_ref[...], preferred_element_type=F32)
    logits = (both[:, :ROUTE_LANES] + both[:, ROUTE_LANES:]
              + jnp.dot(h_lo, w_ref[:, :ROUTE_LANES], preferred_element_type=F32) + b_ref[...])
    lane = lax.broadcasted_iota(jnp.int32, logits.shape, 1)
    m1 = jnp.max(logits, axis=-1, keepdims=True)
    i1 = jnp.min(jnp.where(logits == m1, lane, ROUTE_LANES), axis=-1, keepdims=True)
    rest = jnp.where(lane == i1, MASKED, logits)
    m2 = jnp.max(rest, axis=-1, keepdims=True)
    i2 = jnp.min(jnp.where(rest == m2, lane, ROUTE_LANES), axis=-1, keepdims=True)
    e2 = jnp.exp(m2 - m1)
    g1 = 1.0 / (1.0 + e2)
    out = jnp.where(lane == 0, i1.astype(F32), jnp.where(lane == 1, i2.astype(F32),
                    jnp.where(lane == 2, g1, jnp.where(lane == 3, e2 * g1, 0.0))))
    o_ref[...] = out


def outproj_router(x2, ya, yb, yc, yd_tm, w_out_bf, gain, router_w, router_b, seq, tm=512):
    n_tok = x2.shape[0]
    w = jnp.pad(router_w.astype(F32), ((0, 0), (0, ROUTE_LANES - N_EXPERTS)))
    w_hi = w.astype(BF16)
    w = jnp.concatenate([w_hi, (w - w_hi.astype(F32)).astype(BF16)], axis=1)
    b = jnp.pad(router_b.astype(F32), (0, ROUTE_LANES - N_EXPERTS), constant_values=MASKED).reshape(1, ROUTE_LANES)
    return pl.pallas_call(
        _outproj_router_body,
        grid=(n_tok // tm,),
        in_specs=_mix_out_specs(tm, seq) + [
            _resident((D_MODEL, D_MODEL)), pl.BlockSpec((1, D_MODEL), lambda i: (0, 0)),
            pl.BlockSpec((D_MODEL, 2 * ROUTE_LANES), lambda i: (0, 0)),
            pl.BlockSpec((1, ROUTE_LANES), lambda i: (0, 0))],
        out_specs=[pl.BlockSpec((tm, D_MODEL), lambda i: (i, 0)),
                   pl.BlockSpec((tm, ROUTE_LANES), lambda i: (i, 0))],
        out_shape=[jax.ShapeDtypeStruct((n_tok, D_MODEL), F32),
                   jax.ShapeDtypeStruct((n_tok, ROUTE_LANES), F32)],
        compiler_params=_params(("parallel",)),
        name="outproj_router",
    )(x2, ya, yb, yc, yd_tm, w_out_bf, gain.reshape(1, D_MODEL), w, b)


def _moe_dispatch(experts, n_tok):
    n_assign = n_tok * TOP_K
    n_blocks = n_assign // MOE_ROWS + N_EXPERTS
    cap = n_blocks * MOE_ROWS
    flat_exp = experts.reshape(-1)
    onehot = (flat_exp[:, None] == jnp.arange(N_EXPERTS, dtype=jnp.int32)[None, :]).astype(jnp.int32)
    csum = jnp.cumsum(onehot, axis=0)
    counts = csum[-1]
    rank = jnp.sum(onehot * csum, axis=1) - 1
    padded = (counts + MOE_ROWS - 1) // MOE_ROWS * MOE_ROWS
    pad_end = jnp.cumsum(padded)
    pad_start = pad_end - padded
    dest = pad_start[flat_exp] + rank
    slot_asg = jnp.full((cap,), -1, jnp.int32).at[dest].set(jnp.arange(n_assign, dtype=jnp.int32))
    valid = slot_asg >= 0
    slot_tok = jnp.where(valid, slot_asg // TOP_K, 0)
    spare = n_assign + jnp.cumsum((~valid).astype(jnp.int32)) - 1
    slot_dst = jnp.where(valid, (slot_asg % TOP_K) * n_tok + slot_asg // TOP_K, spare)
    block_exp = jnp.minimum(jnp.searchsorted(pad_end, jnp.arange(n_blocks, dtype=jnp.int32) * MOE_ROWS, side='right'),
                            N_EXPERTS - 1).astype(jnp.int32)
    return (slot_tok.reshape(n_blocks, 1, MOE_ROWS), slot_dst.reshape(n_blocks, 1, MOE_ROWS), block_exp)


def _expert_body(bexp_ref, tok0_ref, tok_next_ref, dst_prev_ref, dst_cur_ref, x_hbm, g_ref, wg_ref, wu_ref, wd_ref,
                 y_hbm, xbuf, obuf, act_scr, sem_in, sem_out):
    del bexp_ref
    i = pl.program_id(0)
    last = pl.num_programs(0) - 1
    cur = i % 2
    nxt = 1 - cur
    ocur = i % OUT_SLOTS
    oprev = (i + OUT_SLOTS - 1) % OUT_SLOTS
    n_chunks = D_FF // FF_CHUNK

    def gather_row(tok_ref, r, slot):
        return pltpu.make_async_copy(x_hbm.at[pl.ds(tok_ref[0, 0, r], 1)], xbuf.at[slot, pl.ds(r, 1)], sem_in.at[slot])

    def scatter_row(dst_ref, r, slot):
        return pltpu.make_async_copy(obuf.at[slot, pl.ds(r, 1)], y_hbm.at[pl.ds(dst_ref[0, 0, r], 1)], sem_out.at[slot])

    def wait_gather(slot):
        pltpu.make_async_copy(x_hbm.at[pl.ds(0, MOE_ROWS)], xbuf.at[slot], sem_in.at[slot]).wait()

    def wait_scatter(slot):
        pltpu.make_async_copy(obuf.at[slot], y_hbm.at[pl.ds(0, MOE_ROWS)], sem_out.at[slot]).wait()

    @pl.when(i == 0)
    def _():
        def start(r, _):
            gather_row(tok0_ref, r, 0).start()
            return 0
        lax.fori_loop(0, MOE_ROWS, start, 0)
        obuf[OUT_SLOTS - 1] = jnp.zeros((MOE_ROWS, D_MODEL), F32)

    wait_gather(cur)

    @pl.when(i >= OUT_SLOTS - 1)
    def _():
        wait_scatter(ocur)

    h = _rms(xbuf[cur], g_ref[...]).astype(BF16)
    out_chunks = D_MODEL // FF_CHUNK
    stages = n_chunks + out_chunks
    gather_stages = stages // 2
    rows_per_gather = -(-MOE_ROWS // gather_stages)
    rows_per_scatter = -(-MOE_ROWS // (stages - gather_stages))

    def start_rows(stage):
        if stage < gather_stages:
            for r in range(stage * rows_per_gather, min((stage + 1) * rows_per_gather, MOE_ROWS)):
                gather_row(tok_next_ref, r, nxt).start()
        else:
            st = stage - gather_stages
            for r in range(st * rows_per_scatter, min((st + 1) * rows_per_scatter, MOE_ROWS)):
                scatter_row(dst_prev_ref, r, oprev).start()

    def zero_after_starts():
        probe = pltpu.bitcast(xbuf[cur, 0:8, 0:FF_CHUNK], jnp.uint32)
        return pltpu.bitcast(lax.shift_right_logical(probe, jnp.uint32(32)), F32)[0:1, :]

    for j in range(n_chunks):
        start_rows(j)
        cols = slice(j * FF_CHUNK, (j + 1) * FF_CHUNK)
        gate = jnp.dot(h, wg_ref[0, :, cols], preferred_element_type=F32)
        up = jnp.dot(h, wu_ref[0, :, cols], preferred_element_type=F32)
        if j < gather_stages:
            up = up + zero_after_starts()
        act_scr[:, cols] = (gate * jax.nn.sigmoid(gate) * up).astype(BF16)

    act = act_scr[...]
    for n in range(out_chunks):
        start_rows(n_chunks + n)
        cols = slice(n * FF_CHUNK, (n + 1) * FF_CHUNK)
        obuf[ocur, :, cols] = jnp.dot(act, wd_ref[0, :, cols], preferred_element_type=F32)

    @pl.when(i == last)
    def _():
        def start(r, _):
            scatter_row(dst_cur_ref, r, ocur).start()
            return 0
        lax.fori_loop(0, MOE_ROWS, start, 0)
        for slot in range(OUT_SLOTS):
            wait_scatter(slot)
        wait_gather(nxt)


def moe_experts(x2, gain, slot_tok, slot_dst, block_exp, wg, wu, wd):
    n_blocks = block_exp.shape[0]
    cap = n_blocks * MOE_ROWS
    spare = (cap + jnp.arange(MOE_ROWS, dtype=jnp.int32)).reshape(1, 1, MOE_ROWS)
    dst_ext = jnp.concatenate([spare, slot_dst], axis=0)
    smem = lambda index: pl.BlockSpec((1, 1, MOE_ROWS), index, memory_space=pltpu.SMEM)
    grid_spec = pltpu.PrefetchScalarGridSpec(
        num_scalar_prefetch=1,
        grid=(n_blocks,),
        in_specs=[smem(lambda i, be: (0, 0, 0)),
                  smem(lambda i, be: (jnp.minimum(i + 1, n_blocks - 1), 0, 0)),
                  smem(lambda i, be: (i, 0, 0)),
                  smem(lambda i, be: (i + 1, 0, 0)),
                  pl.BlockSpec(memory_space=pl.ANY),
                  pl.BlockSpec((1, D_MODEL), lambda i, be: (0, 0)),
                  pl.BlockSpec((1, D_MODEL, D_FF), lambda i, be: (be[i], 0, 0)),
                  pl.BlockSpec((1, D_MODEL, D_FF), lambda i, be: (be[i], 0, 0)),
                  pl.BlockSpec((1, D_FF, D_MODEL), lambda i, be: (be[i], 0, 0))],
        out_specs=pl.BlockSpec(memory_space=pl.ANY),
        scratch_shapes=[pltpu.VMEM((2, MOE_ROWS, D_MODEL), F32), pltpu.VMEM((OUT_SLOTS, MOE_ROWS, D_MODEL), F32),
                        pltpu.VMEM((MOE_ROWS, D_FF), BF16),
                        pltpu.SemaphoreType.DMA((2,)), pltpu.SemaphoreType.DMA((OUT_SLOTS,))],
    )
    return pl.pallas_call(
        _expert_body,
        grid_spec=grid_spec,
        out_shape=jax.ShapeDtypeStruct((cap + MOE_ROWS, D_MODEL), F32),
        compiler_params=_params(("arbitrary",)),
        name="moe_experts",
    )(block_exp, slot_tok, slot_tok, dst_ext, dst_ext, x2, gain.reshape(1, D_MODEL), wg, wu, wd)


def _combine_body(x_ref, y1_ref, y2_ref, r_ref, o_ref):
    route = r_ref[...]
    o_ref[...] = x_ref[...] + route[:, 2:3] * y1_ref[...] + route[:, 3:4] * y2_ref[...]


def moe_combine(x2, y, route, tm=512):
    n_tok = x2.shape[0]
    tiles = n_tok // tm
    return pl.pallas_call(
        _combine_body,
        grid=(tiles,),
        in_specs=[pl.BlockSpec((tm, D_MODEL), lambda i: (i, 0)),
                  pl.BlockSpec((tm, D_MODEL), lambda i: (i, 0)),
                  pl.BlockSpec((tm, D_MODEL), lambda i: (tiles + i, 0)),
                  pl.BlockSpec((tm, ROUTE_LANES), lambda i: (i, 0))],
        out_specs=pl.BlockSpec((tm, D_MODEL), lambda i: (i, 0)),
        out_shape=jax.ShapeDtypeStruct((n_tok, D_MODEL), F32),
        compiler_params=_params(("parallel",)),
        name="moe_combine",
    )(x2, y, y, route)


def moe_ffn(x2, route, gain, wg, wu, wd):
    n_tok = x2.shape[0]
    experts = route[:, :TOP_K].astype(jnp.int32)
    slot_tok, slot_dst, block_exp = _moe_dispatch(experts, n_tok)
    y = moe_experts(x2, gain, slot_tok, slot_dst, block_exp, wg, wu, wd)
    return moe_combine(x2, y, route)


def kernel(x, ln_mix, w_in, w_out, ssd_conv_w, ssd_conv_b, ssd_dt_bias, ssd_a_log, ssd_d, ssd_norm, att_q_norm, att_k_norm, att_out_norm, rwkv_mu, rwkv_w0, rwkv_w2, rwkv_a0, rwkv_a2, rwkv_g2, rwkv_k_k, rwkv_k_a, rwkv_r_k, rwkv_ln_w, rwkv_ln_b, s5_a_re, s5_a_im, s5_b_re, s5_b_im, s5_c_re, s5_c_im, s5_log_dt, s5_d, s5_glu_w, s5_glu_b, s5_out_norm, ln_ffn, ffn_w_gate, ffn_w_up, ffn_w_down, moe_router_w, moe_router_b, moe_w_gate, moe_w_up, moe_w_down):
    bsz, seq, dm = x.shape
    n_tok = bsz * seq
    depth = ln_mix.shape[0]
    x2 = x.reshape(n_tok, dm)
    for layer in range(depth):
        w_seg = _segment_w_in(w_in[layer])
        p_ssd, p_att, p_rwkv, p_s5 = inproj(x2, ln_mix[layer], w_seg, bsz, seq)
        y_a = ssd_mixer(p_ssd, ssd_conv_w[layer], ssd_conv_b[layer], ssd_dt_bias[layer], ssd_a_log[layer],
                        ssd_d[layer], ssd_norm[layer], bsz, seq)
        y_b = attention_mixer(p_att, att_q_norm[layer], att_k_norm[layer], att_out_norm[layer], bsz, seq)
        y_c = rwkv_mixer(p_rwkv, rwkv_mu[layer], rwkv_w0[layer], rwkv_w2[layer], rwkv_a0[layer], rwkv_a2[layer],
                         rwkv_g2[layer], rwkv_k_k[layer], rwkv_k_a[layer], rwkv_r_k[layer], rwkv_ln_w[layer],
                         rwkv_ln_b[layer], bsz, seq)
        y_d = s5_mixer(p_s5, s5_a_re[layer], s5_a_im[layer], s5_b_re[layer], s5_b_im[layer], s5_c_re[layer],
                       s5_c_im[layer], s5_log_dt[layer], s5_d[layer], s5_glu_w[layer], s5_glu_b[layer],
                       s5_out_norm[layer], bsz, seq)
        y_d_tm = y_d.reshape(seq, bsz * GROUP_WIDTH)
        mixed = (x2, y_a, y_b, y_c, y_d_tm, w_out[layer].astype(BF16))
        idx = layer // 2
        if layer % 2 == 0:
            x2 = outproj_ffn(*mixed, ln_ffn[layer], ffn_w_gate[idx].astype(BF16), ffn_w_up[idx].astype(BF16),
                             ffn_w_down[idx].astype(BF16), seq)
        else:
            x2, route = outproj_router(*mixed, ln_ffn[layer], moe_router_w[idx], moe_router_b[idx], seq)
            x2 = moe_ffn(x2, route, ln_ffn[layer], moe_w_gate[idx].astype(BF16), moe_w_up[idx].astype(BF16),
                         moe_w_down[idx].astype(BF16))
    return x2.reshape(bsz, seq, dm)
```

```python
import functools
import math

import jax
import jax.numpy as jnp
from jax import lax
from jax.experimental import pallas as pl
from jax.experimental.pallas import tpu as pltpu

F32 = jnp.float32
BF16 = jnp.bfloat16

D_MODEL = 1024
GROUP_WIDTH = 256
HEAD_DIM = 64
NORM_EPS = 1e-6
D_FF = 2816

SSD_XBC = 512
SSD_SEG = 1024
ATT_SEG = 768
RWKV_SEG = 1024
S5_SEG = 256
SEG_WIDTHS = (SSD_SEG, ATT_SEG, RWKV_SEG, S5_SEG)
SEG_TOTAL = sum(SEG_WIDTHS)

VMEM_LIMIT = 56 * 1024 * 1024


def _params(sem):
    return pltpu.CompilerParams(dimension_semantics=sem, vmem_limit_bytes=VMEM_LIMIT)


def _rms(x, gain):
    ms = jnp.mean(x * x, axis=-1, keepdims=True)
    return x * lax.rsqrt(ms + NORM_EPS) * gain


def _inproj_body(x_ref, g_ref, w_ref, o_ssd, o_att, o_rwkv, o_s5):
    h = _rms(x_ref[...], g_ref[...]).astype(BF16)
    off = 0
    for o_ref in (o_ssd, o_att, o_rwkv, o_s5):
        n = o_ref.shape[-1]
        o_ref[...] = jnp.dot(h, w_ref[:, off:off + n], preferred_element_type=F32)
        off += n


def inproj(x2, gain, w_seg, bsz, seq, tm=512):
    n_tok = x2.shape[0]
    tiles_per_seq = seq // tm
    row = lambda i: (i, 0)
    return pl.pallas_call(
        _inproj_body,
        grid=(n_tok // tm,),
        in_specs=[pl.BlockSpec((tm, D_MODEL), row),
                  pl.BlockSpec((1, D_MODEL), lambda i: (0, 0)),
                  pl.BlockSpec((D_MODEL, SEG_TOTAL), lambda i: (0, 0))],
        out_specs=[pl.BlockSpec((tm, SSD_SEG), row),
                   pl.BlockSpec((tm, ATT_SEG), row),
                   pl.BlockSpec((tm, RWKV_SEG), row),
                   pl.BlockSpec((tm, S5_SEG), lambda i: (i % tiles_per_seq, i // tiles_per_seq))],
        out_shape=[jax.ShapeDtypeStruct((n_tok, SSD_SEG), F32),
                   jax.ShapeDtypeStruct((n_tok, ATT_SEG), F32),
                   jax.ShapeDtypeStruct((n_tok, RWKV_SEG), F32),
                   jax.ShapeDtypeStruct((seq, bsz * S5_SEG), F32)],
        compiler_params=_params(("parallel",)),
        name="inproj",
    )(x2, gain.reshape(1, D_MODEL), w_seg)


def _segment_w_in(w):
    dt_cols = jnp.repeat(w[:, 768:772], HEAD_DIM, axis=1)
    return jnp.concatenate([w[:, :768], dt_cols, w[:, 772:]], axis=1).astype(BF16)


def _mix_out(x_ref, y_refs, wo_ref):
    acc = x_ref[...]
    for j, y_ref in enumerate(y_refs):
        acc = acc + jnp.dot(y_ref[...].astype(BF16), wo_ref[j * GROUP_WIDTH:(j + 1) * GROUP_WIDTH, :],
                            preferred_element_type=F32)
    return acc


def _mix_out_specs(tm, seq):
    tiles_per_seq = seq // tm
    row = lambda i: (i, 0)
    return [pl.BlockSpec((tm, D_MODEL), row),
            pl.BlockSpec((tm, GROUP_WIDTH), row),
            pl.BlockSpec((tm, GROUP_WIDTH), row),
            pl.BlockSpec((tm, GROUP_WIDTH), row),
            pl.BlockSpec((tm, GROUP_WIDTH), lambda i: (i % tiles_per_seq, i // tiles_per_seq))]


def _resident(shape):
    return pl.BlockSpec(shape, lambda i: (0,) * len(shape), pipeline_mode=pl.Buffered(1))


FFN_COLS = 256


def _outproj_ffn_body(x_ref, ya_ref, yb_ref, yc_ref, yd_ref, wo_ref, g_ref, wg_ref, wu_ref, wd_ref, o_ref,
                      xn_scr, act_scr):
    xn = _mix_out(x_ref, (ya_ref, yb_ref, yc_ref, yd_ref), wo_ref)
    xn_scr[...] = xn
    h = _rms(xn, g_ref[...]).astype(BF16)
    for j in range(D_FF // FFN_COLS):
        cols = slice(j * FFN_COLS, (j + 1) * FFN_COLS)
        gate = jnp.dot(h, wg_ref[:, cols], preferred_element_type=F32)
        up = jnp.dot(h, wu_ref[:, cols], preferred_element_type=F32)
        act_scr[:, cols] = (gate * jax.nn.sigmoid(gate) * up).astype(BF16)
    act = act_scr[...]
    for n in range(D_MODEL // FFN_COLS):
        cols = slice(n * FFN_COLS, (n + 1) * FFN_COLS)
        o_ref[:, cols] = xn_scr[:, cols] + jnp.dot(act, wd_ref[:, cols], preferred_element_type=F32)


def outproj_ffn(x2, ya, yb, yc, yd_tm, w_out_bf, gain, wg, wu, wd, seq, tm=512):
    n_tok = x2.shape[0]
    return pl.pallas_call(
        _outproj_ffn_body,
        grid=(n_tok // tm,),
        in_specs=_mix_out_specs(tm, seq) + [
            _resident((D_MODEL, D_MODEL)), pl.BlockSpec((1, D_MODEL), lambda i: (0, 0)),
            _resident((D_MODEL, D_FF)), _resident((D_MODEL, D_FF)), _resident((D_FF, D_MODEL))],
        out_specs=pl.BlockSpec((tm, D_MODEL), lambda i: (i, 0)),
        out_shape=jax.ShapeDtypeStruct((n_tok, D_MODEL), F32),
        scratch_shapes=[pltpu.VMEM((tm, D_MODEL), F32), pltpu.VMEM((tm, D_FF), BF16)],
        compiler_params=_params(("parallel",)),
        name="outproj_ffn",
    )(x2, ya, yb, yc, yd_tm, w_out_bf, gain.reshape(1, D_MODEL), wg, wu, wd)


S5_GROUPS = 16
S5_GROUP = 16
S5_STATE = 64
S5_CH = S5_GROUPS * S5_STATE
S5_BT = 8
S5_TIME_SEG = 32


def _s5_body(u_ref, wb_ref, wc_ref, are_ref, aim_ref, d_ref, gw_ref, gb_ref, gain_ref, o_ref, h_scr, st_scr):
    tt = u_ref.shape[0]

    @pl.when(pl.program_id(1) == 0)
    def _():
        st_scr[...] = jnp.zeros_like(st_scr)

    a_re = jnp.broadcast_to(are_ref[...], (S5_BT, S5_CH))
    a_im = jnp.broadcast_to(aim_ref[...], (S5_BT, S5_CH))
    seg = S5_TIME_SEG
    n_seg = tt // seg

    def inputs(q):
        return u_ref[q * seg:(q + 1) * seg].reshape(seg * S5_BT, GROUP_WIDTH)

    def project_in(q):
        bu = jnp.dot(inputs(q).astype(BF16), wb_ref[...], preferred_element_type=F32)
        h_scr[q * seg:(q + 1) * seg] = bu.reshape(seg, S5_BT, 2 * S5_CH)

    def scan(q, h_re, h_im):
        for t in range(q * seg, (q + 1) * seg):
            n_re = a_re * h_re - a_im * h_im + h_scr[t, :, :S5_CH]
            n_im = a_re * h_im + a_im * h_re + h_scr[t, :, S5_CH:]
            h_scr[t, :, :S5_CH] = n_re
            h_scr[t, :, S5_CH:] = n_im
            h_re, h_im = n_re, n_im
        return h_re, h_im

    def project_out(q):
        hall = h_scr[q * seg:(q + 1) * seg].reshape(seg * S5_BT, 2 * S5_CH)
        y = jnp.dot(hall.astype(BF16), wc_ref[...], preferred_element_type=F32) + d_ref[...] * inputs(q)
        hh = jax.nn.gelu(y)
        gl = jnp.dot(hh.astype(BF16), gw_ref[...], preferred_element_type=F32) + gb_ref[...]
        out = _rms(hh * jax.nn.sigmoid(gl), gain_ref[...])
        o_ref[q * seg:(q + 1) * seg] = out.reshape(seg, S5_BT, GROUP_WIDTH)

    h_re, h_im = st_scr[0], st_scr[1]
    project_in(0)
    for q in range(n_seg):
        if q + 1 < n_seg:
            project_in(q + 1)
        h_re, h_im = scan(q, h_re, h_im)
        project_out(q)
    st_scr[0] = h_re
    st_scr[1] = h_im


def _s5_weights(a_re, a_im, b_re, b_im, c_re, c_im, log_dt):
    dt = jnp.exp(log_dt)[:, None]
    mag = jnp.exp(dt * a_re)
    ab_re = mag * jnp.cos(dt * a_im)
    ab_im = mag * jnp.sin(dt * a_im)
    den = a_re * a_re + a_im * a_im
    f_re = ((ab_re - 1.0) * a_re + ab_im * a_im) / den
    f_im = (ab_im * a_re - (ab_re - 1.0) * a_im) / den
    bb_re = f_re[..., None] * b_re - f_im[..., None] * b_im
    bb_im = f_re[..., None] * b_im + f_im[..., None] * b_re
    eye = jnp.eye(S5_GROUPS, dtype=F32)
    expand_b = lambda m: jnp.einsum('gpc,gh->gchp', m, eye).reshape(GROUP_WIDTH, S5_CH)
    expand_c = lambda m: jnp.einsum('gcp,gh->gphc', m, eye).reshape(S5_CH, GROUP_WIDTH)
    wb = jnp.concatenate([expand_b(bb_re), expand_b(bb_im)], axis=1).astype(BF16)
    wc = jnp.concatenate([expand_c(c_re), -expand_c(c_im)], axis=0).astype(BF16)
    return wb, wc, ab_re.reshape(1, S5_CH), ab_im.reshape(1, S5_CH)


def s5_mixer(u_tm, a_re, a_im, b_re, b_im, c_re, c_im, log_dt, d_skip, glu_w, glu_b, out_gain, bsz, seq, tt=128):
    wb, wc, are, aim = _s5_weights(a_re, a_im, b_re, b_im, c_re, c_im, log_dt)
    u3 = u_tm.reshape(seq, bsz, GROUP_WIDTH)
    const = lambda shape: pl.BlockSpec(shape, lambda i, j: (0,) * len(shape))
    row = lambda v: v.reshape(1, GROUP_WIDTH)
    return pl.pallas_call(
        _s5_body,
        grid=(bsz // S5_BT, seq // tt),
        in_specs=[pl.BlockSpec((tt, S5_BT, GROUP_WIDTH), lambda i, j: (j, i, 0)),
                  const((GROUP_WIDTH, 2 * S5_CH)), const((2 * S5_CH, GROUP_WIDTH)),
                  const((1, S5_CH)), const((1, S5_CH)), const((1, GROUP_WIDTH)),
                  const((GROUP_WIDTH, GROUP_WIDTH)), const((1, GROUP_WIDTH)), const((1, GROUP_WIDTH))],
        out_specs=pl.BlockSpec((tt, S5_BT, GROUP_WIDTH), lambda i, j: (j, i, 0)),
        out_shape=jax.ShapeDtypeStruct((seq, bsz, GROUP_WIDTH), F32),
        scratch_shapes=[pltpu.VMEM((tt, S5_BT, 2 * S5_CH), F32), pltpu.VMEM((2, S5_BT, S5_CH), F32)],
        compiler_params=_params(("parallel", "arbitrary")),
        name="s5_mixer",
    )(u3, wb, wc, are, aim, row(d_skip), glu_w.astype(BF16), row(glu_b), row(out_gain))


SSD_CHUNK = 128
SSD_CONV = 4
SSD_HEADS = 4
SSD_TAIL = 8
HIGHEST = lax.Precision.HIGHEST


def _dot_nt(a, b, **kw):
    return lax.dot_general(a, b, (((1,), (1,)), ((), ())), preferred_element_type=F32, **kw)


def _dot_tn(a, b, **kw):
    return lax.dot_general(a, b, (((0,), (0,)), ((), ())), preferred_element_type=F32, **kw)


def _ssd_body(p_ref, cw_ref, cb_ref, dtb_ref, arow_ref, drow_ref, gain_ref, o_ref, tail_scr, st_scr):
    tb = p_ref.shape[0]
    L = SSD_CHUNK

    @pl.when(pl.program_id(1) == 0)
    def _():
        tail_scr[...] = jnp.zeros_like(tail_scr)
        st_scr[...] = jnp.zeros_like(st_scr)

    raw = p_ref[:, :SSD_XBC]
    ext = jnp.concatenate([tail_scr[...], raw], axis=0)
    conv = cb_ref[...] + cw_ref[SSD_CONV - 1:SSD_CONV, :] * raw
    for j in range(SSD_CONV - 1):
        shifted = pltpu.roll(ext, SSD_CONV - 1 - j, axis=0)[SSD_TAIL:, :]
        conv = conv + cw_ref[j:j + 1, :] * shifted
    tail_scr[...] = raw[tb - SSD_TAIL:, :]
    xc = conv * jax.nn.sigmoid(conv)

    dt = jax.nn.softplus(p_ref[:, 768:1024] + dtb_ref[...])
    a = dt * arow_ref[...]

    ii = lax.broadcasted_iota(jnp.int32, (L, L), 0)
    jj = lax.broadcasted_iota(jnp.int32, (L, L), 1)
    causal = ii >= jj
    tril = causal.astype(BF16)
    low_half = jj < HEAD_DIM
    lane2 = lax.broadcasted_iota(jnp.int32, (L, GROUP_WIDTH), 1)
    lane1 = lax.broadcasted_iota(jnp.int32, (L, L), 1)
    row_grp = lax.broadcasted_iota(jnp.int32, (L, GROUP_WIDTH), 0) // HEAD_DIM
    state_mask = row_grp == lane2 // (2 * HEAD_DIM)

    for c in range(tb // L):
        sl = slice(c * L, (c + 1) * L)
        xs = xc[sl, :GROUP_WIDTH]
        bm = xc[sl, GROUP_WIDTH:GROUP_WIDTH + L]
        cm = xc[sl, GROUP_WIDTH + L:]
        xdt = xs * dt[sl]
        a_hi = a[sl].astype(BF16)
        a_lo = (a[sl] - a_hi.astype(F32)).astype(BF16)
        cs = (jnp.dot(tril, a_hi, preferred_element_type=F32)
              + jnp.dot(tril, a_lo, preferred_element_type=F32))
        cs_end = cs[L - 1:L, :]
        st = st_scr[...]
        y = jnp.exp(cs) * jnp.dot(cm.astype(BF16), st.astype(BF16), preferred_element_type=F32)
        for g in range(2):
            cg = jnp.where((lane1 // HEAD_DIM) == g, cm, 0.0)
            gram = _dot_nt(cg.astype(BF16), bm.astype(BF16))
            pair = cs[:, g * L:(g + 1) * L]
            swapped = pltpu.roll(pair, HEAD_DIM, axis=1)
            for k in range(2):
                h = 2 * g + k
                col = jnp.where(low_half, pair, swapped) if k == 0 else jnp.where(low_half, swapped, pair)
                seg = col - col.T
                decay = jnp.where(causal, jnp.exp(jnp.minimum(seg, 0.0)), 0.0)
                yd = jnp.dot((gram * decay).astype(BF16), xdt.astype(BF16), preferred_element_type=F32)
                y = y + jnp.where((lane2 // HEAD_DIM) == h, yd, 0.0)
        upd = _dot_tn(bm.astype(BF16), (xdt * jnp.exp(cs_end - cs)).astype(BF16))
        st_scr[...] = st * jnp.exp(cs_end) + jnp.where(state_mask, upd, 0.0)
        y = y + drow_ref[...] * xs
        z = p_ref[sl, 512:768]
        o_ref[sl, :] = _rms(y * (z * jax.nn.sigmoid(z)), gain_ref[...])


def ssd_mixer(p_ssd, conv_w, conv_b, dt_bias, a_log, d_skip, norm_gain, bsz, seq, tb=512):
    n_tok = p_ssd.shape[0]
    blocks = seq // tb
    per_head = lambda v: jnp.repeat(v.astype(F32), HEAD_DIM).reshape(1, GROUP_WIDTH)
    const = lambda shape: pl.BlockSpec(shape, lambda b, j: (0,) * len(shape))
    return pl.pallas_call(
        _ssd_body,
        grid=(bsz, blocks),
        in_specs=[pl.BlockSpec((tb, SSD_SEG), lambda b, j: (b * blocks + j, 0)),
                  const((SSD_CONV, SSD_XBC)), const((1, SSD_XBC)), const((1, GROUP_WIDTH)),
                  const((1, GROUP_WIDTH)), const((1, GROUP_WIDTH)), const((1, GROUP_WIDTH))],
        out_specs=pl.BlockSpec((tb, GROUP_WIDTH), lambda b, j: (b * blocks + j, 0)),
        out_shape=jax.ShapeDtypeStruct((n_tok, GROUP_WIDTH), F32),
        scratch_shapes=[pltpu.VMEM((SSD_TAIL, SSD_XBC), F32), pltpu.VMEM((2 * HEAD_DIM, GROUP_WIDTH), F32)],
        compiler_params=_params(("parallel", "arbitrary")),
        name="ssd_mixer",
    )(p_ssd, conv_w, conv_b.reshape(1, SSD_XBC), per_head(dt_bias), per_head(-jnp.exp(a_log)), per_head(d_skip),
      norm_gain.reshape(1, GROUP_WIDTH))


ATT_HEADS = 4
ATT_BLOCK = 128
DILATED_PATTERNS = ((128, 1), (512, 4), (2048, 16))
MASKED = -1e30
ATT_PAR = 4


def _att_bias():
    slopes = [2.0 ** (-8.0 / ATT_HEADS * (h + 1)) for h in range(ATT_HEADS)]
    q_idx = jnp.arange(ATT_BLOCK)[:, None]
    k_idx = jnp.arange(2 * ATT_BLOCK)[None, :] - ATT_BLOCK
    rel = q_idx - k_idx
    out = []
    for window, dilation in DILATED_PATTERNS:
        span = window // dilation
        valid = (rel >= 0) & (rel <= span)
        per_head = [jnp.where(valid, -s * (rel * dilation).astype(F32), MASKED) for s in slopes]
        out.append(jnp.stack(per_head))
    return jnp.stack(out)


def _ld2(scr, rows):
    return jnp.concatenate([scr[0, rows, :], scr[1, rows, :]], axis=1)


def _st2(scr, rows, val):
    scr[0, rows, :] = val[:, :ATT_BLOCK]
    scr[1, rows, :] = val[:, ATT_BLOCK:]


def _att_body(qkv_ref, gmat_ref, qg_ref, kg_ref, og_ref, bias_ref, o_ref, q_scr, k_scr, v_scr, u_scr, m_scr, s_scr):
    seq = qkv_ref.shape[0]
    blk = ATT_BLOCK
    lane = lax.broadcasted_iota(jnp.int32, (blk, GROUP_WIDTH), 1) // HEAD_DIM
    every = pl.ds(0, seq)

    def head_norm(t, gain):
        sq = t * t
        hi = sq.astype(BF16)
        lo = (sq - hi.astype(F32)).astype(BF16)
        ms = (jnp.dot(hi, gmat_ref[...], preferred_element_type=F32)
              + jnp.dot(lo, gmat_ref[...], preferred_element_type=F32))
        return t * lax.rsqrt(ms + NORM_EPS) * gain

    _st2(q_scr, every, head_norm(qkv_ref[:, :GROUP_WIDTH], qg_ref[...]) * (1.0 / math.sqrt(HEAD_DIM)))
    _st2(k_scr, every, head_norm(qkv_ref[:, GROUP_WIDTH:2 * GROUP_WIDTH], kg_ref[...]))
    _st2(v_scr, every, qkv_ref[:, 2 * GROUP_WIDTH:])

    prev_cols = lax.broadcasted_iota(jnp.int32, (1, 2 * blk), 1) < blk
    heads = range(ATT_HEADS)

    def blocks(pat, specs):
        pairs = [(b, h) for b in range(len(specs)) for h in heads]
        q, keys, vals, gate = [], [], [], []
        for rows, prev_rows, has_prev in specs:
            q.append(_ld2(q_scr, rows))
            if prev_rows is None:
                keys.append(_ld2(k_scr, rows).astype(BF16))
                vals.append(_ld2(v_scr, rows).astype(BF16))
                gate.append(None)
            else:
                keys.append(jnp.concatenate([_ld2(k_scr, prev_rows), _ld2(k_scr, rows)], axis=0).astype(BF16))
                vals.append(jnp.concatenate([_ld2(v_scr, prev_rows), _ld2(v_scr, rows)], axis=0).astype(BF16))
                gate.append(jnp.where(jnp.logical_or(has_prev, jnp.logical_not(prev_cols)), 0.0, MASKED))
        bias = {(b, h): (bias_ref[pat, h, :, blk:] if gate[b] is None else bias_ref[pat, h] + gate[b])
                for b, h in pairs}
        qh = {(b, h): jnp.where(lane == h, q[b], 0.0).astype(BF16) for b, h in pairs}
        sc = {(b, h): _dot_nt(qh[b, h], keys[b]) + bias[b, h] for b, h in pairs}
        m = {bh: jnp.max(sc[bh], axis=-1, keepdims=True) for bh in pairs}
        p = {bh: jnp.exp(sc[bh] - m[bh]) for bh in pairs}
        den = {bh: jnp.sum(p[bh], axis=-1, keepdims=True) for bh in pairs}
        u = {(b, h): jnp.dot(p[b, h].astype(BF16), vals[b], preferred_element_type=F32) for b, h in pairs}
        out = []
        for b in range(len(specs)):
            u_all, m_all, s_all = u[b, 0], m[b, 0], den[b, 0]
            for h in heads[1:]:
                sel = lane == h
                u_all = jnp.where(sel, u[b, h], u_all)
                m_all = jnp.where(sel, m[b, h], m_all)
                s_all = jnp.where(sel, den[b, h], s_all)
            out.append((u_all, m_all, s_all))
        return out

    def merge(rows, u, m, s):
        m0 = _ld2(m_scr, rows)
        m_new = jnp.maximum(m0, m)
        alpha = jnp.exp(m0 - m_new)
        beta = jnp.exp(m - m_new)
        return alpha * _ld2(u_scr, rows) + beta * u, m_new, alpha * _ld2(s_scr, rows) + beta * s

    def save(rows, u, m, s):
        _st2(u_scr, rows, u)
        _st2(m_scr, rows, m)
        _st2(s_scr, rows, s)

    par = ATT_PAR
    n_iter = 16 // par

    def body16(i, _):
        specs = [(pl.ds(i + n_iter * b, blk, stride=16), None, None) for b in range(par)]
        for spec, res in zip(specs, blocks(2, specs)):
            save(spec[0], *res)
        return 0
    lax.fori_loop(0, n_iter, body16, 0)

    def body4(i, _):
        specs = []
        for b in range(par):
            idx = i + n_iter * b
            r = idx % 4
            n = idx // 4
            specs.append((pl.ds(n * (4 * blk) + r, blk, stride=4),
                          pl.ds(jnp.maximum(n - 1, 0) * (4 * blk) + r, blk, stride=4), n > 0))
        for spec, res in zip(specs, blocks(1, specs)):
            save(spec[0], *merge(spec[0], *res))
        return 0
    lax.fori_loop(0, n_iter, body4, 0)

    def body1(i, _):
        specs = []
        for b in range(par):
            n = i + (seq // blk // par) * b
            specs.append((pl.ds(pl.multiple_of(n * blk, blk), blk),
                          pl.ds(pl.multiple_of(jnp.maximum(n - 1, 0) * blk, blk), blk), n > 0))
        for spec, res in zip(specs, blocks(0, specs)):
            u, m, s = merge(spec[0], *res)
            o_ref[spec[0], :] = _rms(u / s, og_ref[...])
        return 0
    lax.fori_loop(0, seq // blk // par, body1, 0)


def attention_mixer(qkv, q_gain, k_gain, out_gain, bsz, seq):
    n_tok = qkv.shape[0]
    head_id = jnp.arange(GROUP_WIDTH) // HEAD_DIM
    gmat = ((head_id[:, None] == head_id[None, :]).astype(F32) / HEAD_DIM).astype(BF16)
    tile = lambda g: jnp.tile(g.astype(F32), ATT_HEADS).reshape(1, GROUP_WIDTH)
    const = lambda shape: pl.BlockSpec(shape, lambda b: (0,) * len(shape))
    return pl.pallas_call(
        _att_body,
        grid=(bsz,),
        in_specs=[pl.BlockSpec((seq, ATT_SEG), lambda b: (b, 0)),
                  const((GROUP_WIDTH, GROUP_WIDTH)), const((1, GROUP_WIDTH)), const((1, GROUP_WIDTH)),
                  const((1, GROUP_WIDTH)), const((3, ATT_HEADS, ATT_BLOCK, 2 * ATT_BLOCK))],
        out_specs=pl.BlockSpec((seq, GROUP_WIDTH), lambda b: (b, 0)),
        out_shape=jax.ShapeDtypeStruct((n_tok, GROUP_WIDTH), F32),
        scratch_shapes=[pltpu.VMEM((2, seq, ATT_BLOCK), F32) for _ in range(6)],
        compiler_params=_params(("parallel",)),
        name="attention_mixer",
    )(qkv, gmat, tile(q_gain), tile(k_gain), out_gain.reshape(1, GROUP_WIDTH), _att_bias())


RWKV_HEADS = 4
RWKV_CHUNK = 64
RWKV_DECAY_SCALE = 0.606531
RWKV_GN_EPS = 64e-5
RWKV_MAT = RWKV_HEADS * RWKV_CHUNK


def _per_head_blocks(x):
    lane_head = lax.broadcasted_iota(jnp.int32, x.shape, 1) // HEAD_DIM
    return jnp.concatenate([jnp.where(lane_head == h, x, 0.0) for h in range(RWKV_HEADS)], axis=0)


def _rwkv_body(p_ref, mu_ref, w0_ref, w2_ref, a0_ref, a2_ref, g2_ref, kk_ref, ka_ref, rk_ref, lnw_ref, lnb_ref,
               hsum_ref, o_ref, prev_scr, z_scr):
    nb, tb = p_ref.shape[0], p_ref.shape[1]
    L = RWKV_CHUNK

    def mb(a, b):
        return jnp.dot(a.astype(BF16), b.astype(BF16), preferred_element_type=F32)

    def split(x):
        hi = x.astype(BF16)
        return hi, (x - hi.astype(F32)).astype(BF16)

    def head_sum(x):
        return jnp.dot(x.astype(BF16), hsum_ref[...], preferred_element_type=F32)

    @pl.when(pl.program_id(1) == 0)
    def _():
        prev_scr[...] = jnp.zeros_like(prev_scr)
        z_scr[...] = jnp.zeros_like(z_scr)

    ii = lax.broadcasted_iota(jnp.int32, (L, RWKV_MAT), 0)
    jj = lax.broadcasted_iota(jnp.int32, (L, RWKV_MAT), 1) % L
    eye_all = (ii == jj).astype(F32)
    ti = lax.broadcasted_iota(jnp.int32, (L, L), 0)
    tj = lax.broadcasted_iota(jnp.int32, (L, L), 1)
    tril = (ti >= tj).astype(BF16)
    first_row = lax.broadcasted_iota(jnp.int32, (L, RWKV_SEG), 0) == 0
    zi = lax.broadcasted_iota(jnp.int32, (RWKV_MAT, RWKV_MAT), 0) // HEAD_DIM
    zj = lax.broadcasted_iota(jnp.int32, (RWKV_MAT, RWKV_MAT), 1) // HEAD_DIM
    same_head = zi == zj

    def each(fn, *lists):
        return [fn(*args) for args in zip(*lists)]

    def chunks(c, _):
        seqs = list(range(nb))
        rows = pl.ds(pl.multiple_of(c * L, L), L)
        pc = [p_ref[s, rows, :] for s in seqs]
        prev = [jnp.where(first_row, prev_scr[s], pltpu.roll(pc[s], 1, axis=0)) for s in seqs]
        for s in seqs:
            prev_scr[s] = pc[s][L - 1:L, :]
        pf = each(lambda x, xp: x + mu_ref[...] * (xp - x), pc, prev)
        r = [x[:, :GROUP_WIDTH] for x in pf]
        k = [x[:, GROUP_WIDTH:2 * GROUP_WIDTH] for x in pf]
        v = [x[:, 2 * GROUP_WIDTH:3 * GROUP_WIDTH] for x in pf]
        wa = [x[:, 3 * GROUP_WIDTH:3 * GROUP_WIDTH + 128] for x in pf]
        gd = [x[:, 3 * GROUP_WIDTH + 128:] for x in pf]
        log_w = each(lambda x: -RWKV_DECAY_SCALE * jax.nn.sigmoid(
            w0_ref[...] + jnp.dot(jnp.tanh(x).astype(BF16), w2_ref[...], preferred_element_type=F32)), wa)
        alr = each(lambda x: jax.nn.sigmoid(
            a0_ref[...] + jnp.dot(x.astype(BF16), a2_ref[...], preferred_element_type=F32)), wa)
        g = each(lambda x: jnp.dot(jax.nn.sigmoid(x).astype(BF16), g2_ref[...], preferred_element_type=F32), gd)
        kk = each(lambda x: x * kk_ref[...], k)
        kk_ss = each(lambda x: head_sum(x * x), kk)
        kk = each(lambda x, ss: x * lax.rsqrt(ss + 1e-12), kk, kk_ss)
        k2 = each(lambda x, a: x * (1.0 + (a - 1.0) * ka_ref[...]), k, alr)

        def cumsum(lw):
            hi, lo = split(lw)
            return (jnp.dot(tril, hi, preferred_element_type=F32) + jnp.dot(tril, lo, preferred_element_type=F32))
        cs = each(cumsum, log_w)
        gam = each(jnp.exp, cs)
        gam_inv = each(lambda x: jnp.exp(-x), cs)
        a_hat = each(lambda x, c_, lw: -x * jnp.exp(c_ - lw), kk, cs, log_w)
        b_hat = each(lambda x, a, gi: x * a * gi, kk, alr, gam_inv)
        k_hat = each(lambda x, gi: x * gi, k2, gam_inv)
        r_hat = each(lambda x, gm: x * gm, r, gam)

        gram = each(lambda a, rr, b, kh: _dot_nt(
            jnp.concatenate([a, rr], axis=0).astype(BF16),
            jnp.concatenate([_per_head_blocks(b), _per_head_blocks(kh)], axis=0).astype(BF16)),
            a_hat, r_hat, b_hat, k_hat)
        n_all = [jnp.where(ii > jj, x[:L, :RWKV_MAT], 0.0) for x in gram]
        m_all = [jnp.where(ii > jj, x[:L, RWKV_MAT:], 0.0) for x in gram]
        gb_all = [jnp.where(ii >= jj, x[L:, :RWKV_MAT], 0.0) for x in gram]
        gk_all = [jnp.where(ii >= jj, x[L:, RWKV_MAT:], 0.0) for x in gram]

        w_all = [eye_all + n for n in n_all]
        n_pow = each(lambda n: mb(n, _per_head_blocks(n)), n_all)
        for _ in range(4):
            res = each(lambda n, w: mb(jnp.concatenate([n, w], axis=0), _per_head_blocks(n)), n_pow, w_all)
            n_pow = [x[:L] for x in res]
            w_all = each(lambda w, x: w + x[L:], w_all, res)
        w_all = each(lambda w, n: w + mb(w, _per_head_blocks(n)), w_all, n_pow)

        t1 = each(lambda m, vv: mb(m, _per_head_blocks(vv)), m_all, v)
        at_vt = each(lambda w, a, t: mb(w, jnp.concatenate([_per_head_blocks(a), _per_head_blocks(t)], axis=1)),
                     w_all, a_hat, t1)
        a_til = [x[:, :RWKV_MAT] for x in at_vt]
        v_til = [x[:, RWKV_MAT:] for x in at_vt]

        z = [z_scr[s] for s in seqs]
        ax_rx = each(lambda a, rr, zz: _dot_nt(jnp.concatenate([a, rr], axis=0).astype(BF16), zz.astype(BF16)),
                     a_til, r_hat, z)
        u = each(lambda x, vt: x[:L] + vt, ax_rx, v_til)
        y = each(lambda x, gb, gk, uu, vv: x[L:] + mb(
            jnp.concatenate([gb, gk], axis=1),
            jnp.concatenate([_per_head_blocks(uu), _per_head_blocks(vv)], axis=0)), ax_rx, gb_all, gk_all, u, v)
        d = each(lambda uu, vv, b, kh: _dot_tn(jnp.concatenate([uu, vv], axis=0).astype(BF16),
                                               jnp.concatenate([b, kh], axis=0).astype(BF16)), u, v, b_hat, k_hat)
        for s in seqs:
            z_scr[s] = (z[s] + jnp.where(same_head, d[s], 0.0)) * gam[s][L - 1:L, :]

        mean = each(lambda x: head_sum(x) * (1.0 / HEAD_DIM), y)
        yc = each(lambda x, m: x - m, y, mean)
        var = each(lambda x: head_sum(x * x) * (1.0 / HEAD_DIM), yc)
        yn = each(lambda x, vr: x * lax.rsqrt(vr + RWKV_GN_EPS) * lnw_ref[...] + lnb_ref[...], yc, var)
        bonus = each(lambda rr, kx, vv: head_sum(rr * kx * rk_ref[...]) * vv, r, k2, v)
        for s in seqs:
            o_ref[s, rows, :] = (yn[s] + bonus[s]) * g[s]
        return 0

    lax.fori_loop(0, tb // L, chunks, 0)


def rwkv_mixer(p_rwkv, mu, w0, w2, a0, a2, g2, k_k, k_a, r_k, ln_w, ln_b, bsz, seq, tb=256, nb=8):
    n_tok = p_rwkv.shape[0]
    nb = nb if bsz % nb == 0 else 1
    row = lambda t: t.reshape(1, -1).astype(F32)
    w2p = jnp.concatenate([w2, jnp.zeros_like(a2)], axis=0).astype(BF16)
    a2p = jnp.concatenate([jnp.zeros_like(w2), a2], axis=0).astype(BF16)
    head_id = jnp.arange(GROUP_WIDTH) // HEAD_DIM
    hsum = (head_id[:, None] == head_id[None, :]).astype(BF16)
    const = lambda shape: pl.BlockSpec(shape, lambda b, j: (0,) * len(shape))
    vec = const((1, GROUP_WIDTH))
    return pl.pallas_call(
        _rwkv_body,
        grid=(bsz // nb, seq // tb),
        in_specs=[pl.BlockSpec((nb, tb, RWKV_SEG), lambda b, j: (b, j, 0)),
                  const((1, RWKV_SEG)), vec, const((128, GROUP_WIDTH)), vec, const((128, GROUP_WIDTH)),
                  const((128, GROUP_WIDTH)), vec, vec, vec, vec, vec, const((GROUP_WIDTH, GROUP_WIDTH))],
        out_specs=pl.BlockSpec((nb, tb, GROUP_WIDTH), lambda b, j: (b, j, 0)),
        out_shape=jax.ShapeDtypeStruct((bsz, seq, GROUP_WIDTH), F32),
        scratch_shapes=[pltpu.VMEM((nb, 1, RWKV_SEG), F32), pltpu.VMEM((nb, RWKV_MAT, RWKV_MAT), F32)],
        compiler_params=_params(("parallel", "arbitrary")),
        name="rwkv_mixer",
    )(p_rwkv.reshape(bsz, seq, RWKV_SEG), row(mu), row(w0), w2p, row(a0), a2p, g2.astype(BF16), row(k_k), row(k_a),
      row(r_k), row(ln_w), row(ln_b), hsum).reshape(n_tok, GROUP_WIDTH)


N_EXPERTS = 8
TOP_K = 2
MOE_ROWS = 256
ROUTE_LANES = 128
FF_CHUNK = 256
OUT_SLOTS = 3


def _outproj_router_body(x_ref, ya_ref, yb_ref, yc_ref, yd_ref, wo_ref, g_ref, w_ref, b_ref, xn_ref, o_ref):
    xn = _mix_out(x_ref, (ya_ref, yb_ref, yc_ref, yd_ref), wo_ref)
    xn_ref[...] = xn
    h = _rms(xn, g_ref[...])
    h_hi = h.astype(BF16)
    h_lo = (h - h_hi.astype(F32)).astype(BF16)
    both = jnp.dot(h_hi, w_ref[...], preferred_element_type=F32)
    logits = (both[:, :ROUTE_LANES] + both[:, ROUTE_LANES:]
              + jnp.dot(h_lo, w_ref[:, :ROUTE_LANES], preferred_element_type=F32) + b_ref[...])
    lane = lax.broadcasted_iota(jnp.int32, logits.shape, 1)
    m1 = jnp.max(logits, axis=-1, keepdims=True)
    i1 = jnp.min(jnp.where(logits == m1, lane, ROUTE_LANES), axis=-1, keepdims=True)
    rest = jnp.where(lane == i1, MASKED, logits)
    m2 = jnp.max(rest, axis=-1, keepdims=True)
    i2 = jnp.min(jnp.where(rest == m2, lane, ROUTE_LANES), axis=-1, keepdims=True)
    e2 = jnp.exp(m2 - m1)
    g1 = 1.0 / (1.0 + e2)
    out = jnp.where(lane == 0, i1.astype(F32), jnp.where(lane == 1, i2.astype(F32),
                    jnp.where(lane == 2, g1, jnp.where(lane == 3, e2 * g1, 0.0))))
    o_ref[...] = out


def outproj_router(x2, ya, yb, yc, yd_tm, w_out_bf, gain, router_w, router_b, seq, tm=512):
    n_tok = x2.shape[0]
    w = jnp.pad(router_w.astype(F32), ((0, 0), (0, ROUTE_LANES - N_EXPERTS)))
    w_hi = w.astype(BF16)
    w = jnp.concatenate([w_hi, (w - w_hi.astype(F32)).astype(BF16)], axis=1)
    b = jnp.pad(router_b.astype(F32), (0, ROUTE_LANES - N_EXPERTS), constant_values=MASKED).reshape(1, ROUTE_LANES)
    return pl.pallas_call(
        _outproj_router_body,
        grid=(n_tok // tm,),
        in_specs=_mix_out_specs(tm, seq) + [
            _resident((D_MODEL, D_MODEL)), pl.BlockSpec((1, D_MODEL), lambda i: (0, 0)),
            pl.BlockSpec((D_MODEL, 2 * ROUTE_LANES), lambda i: (0, 0)),
            pl.BlockSpec((1, ROUTE_LANES), lambda i: (0, 0))],
        out_specs=[pl.BlockSpec((tm, D_MODEL), lambda i: (i, 0)),
                   pl.BlockSpec((tm, ROUTE_LANES), lambda i: (i, 0))],
        out_shape=[jax.ShapeDtypeStruct((n_tok, D_MODEL), F32),
                   jax.ShapeDtypeStruct((n_tok, ROUTE_LANES), F32)],
        compiler_params=_params(("parallel",)),
        name="outproj_router",
    )(x2, ya, yb, yc, yd_tm, w_out_bf, gain.reshape(1, D_MODEL), w, b)


def _moe_dispatch(experts, n_tok):
    n_assign = n_tok * TOP_K
    n_blocks = n_assign // MOE_ROWS + N_EXPERTS
    cap = n_blocks * MOE_ROWS
    flat_exp = experts.reshape(-1)
    onehot = (flat_exp[:, None] == jnp.arange(N_EXPERTS, dtype=jnp.int32)[None, :]).astype(jnp.int32)
    csum = jnp.cumsum(onehot, axis=0)
    counts = csum[-1]
    rank = jnp.sum(onehot * csum, axis=1) - 1
    padded = (counts + MOE_ROWS - 1) // MOE_ROWS * MOE_ROWS
    pad_end = jnp.cumsum(padded)
    pad_start = pad_end - padded
    dest = pad_start[flat_exp] + rank
    slot_asg = jnp.full((cap,), -1, jnp.int32).at[dest].set(
        jnp.arange(n_assign, dtype=jnp.int32), unique_indices=True, mode='promise_in_bounds')
    valid = slot_asg >= 0
    slot_tok = jnp.where(valid, slot_asg // TOP_K, 0)
    spare = n_assign + jnp.cumsum((~valid).astype(jnp.int32)) - 1
    slot_dst = jnp.where(valid, (slot_asg % TOP_K) * n_tok + slot_asg // TOP_K, spare)
    block_exp = jnp.minimum(jnp.searchsorted(pad_end, jnp.arange(n_blocks, dtype=jnp.int32) * MOE_ROWS, side='right'),
                            N_EXPERTS - 1).astype(jnp.int32)
    return (slot_tok.reshape(n_blocks, 1, MOE_ROWS), slot_dst.reshape(n_blocks, 1, MOE_ROWS), block_exp)


def _expert_body(bexp_ref, tok0_ref, tok_next_ref, dst_prev_ref, dst_cur_ref, x_hbm, g_ref, wg_ref, wu_ref, wd_ref,
                 y_hbm, xbuf, obuf, act_scr, sem_in, sem_out):
    del bexp_ref
    i = pl.program_id(0)
    last = pl.num_programs(0) - 1
    cur = i % 2
    nxt = 1 - cur
    ocur = i % OUT_SLOTS
    oprev = (i + OUT_SLOTS - 1) % OUT_SLOTS
    n_chunks = D_FF // FF_CHUNK

    def gather_row(tok_ref, r, slot):
        return pltpu.make_async_copy(x_hbm.at[pl.ds(tok_ref[0, 0, r], 1)], xbuf.at[slot, pl.ds(r, 1)], sem_in.at[slot])

    def scatter_row(dst_ref, r, slot):
        return pltpu.make_async_copy(obuf.at[slot, pl.ds(r, 1)], y_hbm.at[pl.ds(dst_ref[0, 0, r], 1)], sem_out.at[slot])

    def wait_gather(slot):
        pltpu.make_async_copy(x_hbm.at[pl.ds(0, MOE_ROWS)], xbuf.at[slot], sem_in.at[slot]).wait()

    def wait_scatter(slot):
        pltpu.make_async_copy(obuf.at[slot], y_hbm.at[pl.ds(0, MOE_ROWS)], sem_out.at[slot]).wait()

    @pl.when(i == 0)
    def _():
        def start(r, _):
            gather_row(tok0_ref, r, 0).start()
            return 0
        lax.fori_loop(0, MOE_ROWS, start, 0)
        obuf[OUT_SLOTS - 1] = jnp.zeros((MOE_ROWS, D_MODEL), F32)

    wait_gather(cur)

    @pl.when(i >= OUT_SLOTS - 1)
    def _():
        wait_scatter(ocur)

    h = _rms(xbuf[cur], g_ref[...]).astype(BF16)
    out_chunks = D_MODEL // FF_CHUNK
    stages = n_chunks + out_chunks
    gather_stages = stages // 2
    rows_per_gather = -(-MOE_ROWS // gather_stages)
    rows_per_scatter = -(-MOE_ROWS // (stages - gather_stages))

    def start_rows(stage):
        if stage < gather_stages:
            for r in range(stage * rows_per_gather, min((stage + 1) * rows_per_gather, MOE_ROWS)):
                gather_row(tok_next_ref, r, nxt).start()
        else:
            st = stage - gather_stages
            for r in range(st * rows_per_scatter, min((st + 1) * rows_per_scatter, MOE_ROWS)):
                scatter_row(dst_prev_ref, r, oprev).start()

    def zero_after_starts():
        probe = pltpu.bitcast(xbuf[cur, 0:8, 0:FF_CHUNK], jnp.uint32)
        return pltpu.bitcast(lax.shift_right_logical(probe, jnp.uint32(32)), F32)[0:1, :]

    for j in range(n_chunks):
        start_rows(j)
        cols = slice(j * FF_CHUNK, (j + 1) * FF_CHUNK)
        gate = jnp.dot(h, wg_ref[0, :, cols], preferred_element_type=F32)
        up = jnp.dot(h, wu_ref[0, :, cols], preferred_element_type=F32)
        if j < gather_stages:
            up = up + zero_after_starts()
        act_scr[:, cols] = (gate * jax.nn.sigmoid(gate) * up).astype(BF16)

    act = act_scr[...]
    for n in range(out_chunks):
        start_rows(n_chunks + n)
        cols = slice(n * FF_CHUNK, (n + 1) * FF_CHUNK)
        obuf[ocur, :, cols] = jnp.dot(act, wd_ref[0, :, cols], preferred_element_type=F32)

    @pl.when(i == last)
    def _():
        def start(r, _):
            scatter_row(dst_cur_ref, r, ocur).start()
            return 0
        lax.fori_loop(0, MOE_ROWS, start, 0)
        for slot in range(OUT_SLOTS):
            wait_scatter(slot)
        wait_gather(nxt)


def moe_experts(x2, gain, slot_tok, slot_dst, block_exp, wg, wu, wd):
    n_blocks = block_exp.shape[0]
    cap = n_blocks * MOE_ROWS
    spare = (cap + jnp.arange(MOE_ROWS, dtype=jnp.int32)).reshape(1, 1, MOE_ROWS)
    dst_ext = jnp.concatenate([spare, slot_dst], axis=0)
    smem = lambda index: pl.BlockSpec((1, 1, MOE_ROWS), index, memory_space=pltpu.SMEM)
    grid_spec = pltpu.PrefetchScalarGridSpec(
        num_scalar_prefetch=1,
        grid=(n_blocks,),
        in_specs=[smem(lambda i, be: (0, 0, 0)),
                  smem(lambda i, be: (jnp.minimum(i + 1, n_blocks - 1), 0, 0)),
                  smem(lambda i, be: (i, 0, 0)),
                  smem(lambda i, be: (i + 1, 0, 0)),
                  pl.BlockSpec(memory_space=pl.ANY),
                  pl.BlockSpec((1, D_MODEL), lambda i, be: (0, 0)),
                  pl.BlockSpec((1, D_MODEL, D_FF), lambda i, be: (be[i], 0, 0)),
                  pl.BlockSpec((1, D_MODEL, D_FF), lambda i, be: (be[i], 0, 0)),
                  pl.BlockSpec((1, D_FF, D_MODEL), lambda i, be: (be[i], 0, 0))],
        out_specs=pl.BlockSpec(memory_space=pl.ANY),
        scratch_shapes=[pltpu.VMEM((2, MOE_ROWS, D_MODEL), F32), pltpu.VMEM((OUT_SLOTS, MOE_ROWS, D_MODEL), F32),
                        pltpu.VMEM((MOE_ROWS, D_FF), BF16),
                        pltpu.SemaphoreType.DMA((2,)), pltpu.SemaphoreType.DMA((OUT_SLOTS,))],
    )
    return pl.pallas_call(
        _expert_body,
        grid_spec=grid_spec,
        out_shape=jax.ShapeDtypeStruct((cap + MOE_ROWS, D_MODEL), F32),
        compiler_params=_params(("arbitrary",)),
        name="moe_experts",
    )(block_exp, slot_tok, slot_tok, dst_ext, dst_ext, x2, gain.reshape(1, D_MODEL), wg, wu, wd)


def _combine_body(x_ref, y1_ref, y2_ref, r_ref, o_ref):
    route = r_ref[...]
    o_ref[...] = x_ref[...] + route[:, 2:3] * y1_ref[...] + route[:, 3:4] * y2_ref[...]


def moe_combine(x2, y, route, tm=512):
    n_tok = x2.shape[0]
    tiles = n_tok // tm
    return pl.pallas_call(
        _combine_body,
        grid=(tiles,),
        in_specs=[pl.BlockSpec((tm, D_MODEL), lambda i: (i, 0)),
                  pl.BlockSpec((tm, D_MODEL), lambda i: (i, 0)),
                  pl.BlockSpec((tm, D_MODEL), lambda i: (tiles + i, 0)),
                  pl.BlockSpec((tm, ROUTE_LANES), lambda i: (i, 0))],
        out_specs=pl.BlockSpec((tm, D_MODEL), lambda i: (i, 0)),
        out_shape=jax.ShapeDtypeStruct((n_tok, D_MODEL), F32),
        compiler_params=_params(("parallel",)),
        name="moe_combine",
    )(x2, y, y, route)


def moe_ffn(x2, route, gain, wg, wu, wd):
    n_tok = x2.shape[0]
    experts = route[:, :TOP_K].astype(jnp.int32)
    slot_tok, slot_dst, block_exp = _moe_dispatch(experts, n_tok)
    y = moe_experts(x2, gain, slot_tok, slot_dst, block_exp, wg, wu, wd)
    return moe_combine(x2, y, route)


def kernel(x, ln_mix, w_in, w_out, ssd_conv_w, ssd_conv_b, ssd_dt_bias, ssd_a_log, ssd_d, ssd_norm, att_q_norm, att_k_norm, att_out_norm, rwkv_mu, rwkv_w0, rwkv_w2, rwkv_a0, rwkv_a2, rwkv_g2, rwkv_k_k, rwkv_k_a, rwkv_r_k, rwkv_ln_w, rwkv_ln_b, s5_a_re, s5_a_im, s5_b_re, s5_b_im, s5_c_re, s5_c_im, s5_log_dt, s5_d, s5_glu_w, s5_glu_b, s5_out_norm, ln_ffn, ffn_w_gate, ffn_w_up, ffn_w_down, moe_router_w, moe_router_b, moe_w_gate, moe_w_up, moe_w_down):
    bsz, seq, dm = x.shape
    n_tok = bsz * seq
    depth = ln_mix.shape[0]
    x2 = x.reshape(n_tok, dm)
    for layer in range(depth):
        w_seg = _segment_w_in(w_in[layer])
        p_ssd, p_att, p_rwkv, p_s5 = inproj(x2, ln_mix[layer], w_seg, bsz, seq)
        y_a = ssd_mixer(p_ssd, ssd_conv_w[layer], ssd_conv_b[layer], ssd_dt_bias[layer], ssd_a_log[layer],
                        ssd_d[layer], ssd_norm[layer], bsz, seq)
        y_b = attention_mixer(p_att, att_q_norm[layer], att_k_norm[layer], att_out_norm[layer], bsz, seq)
        y_c = rwkv_mixer(p_rwkv, rwkv_mu[layer], rwkv_w0[layer], rwkv_w2[layer], rwkv_a0[layer], rwkv_a2[layer],
                         rwkv_g2[layer], rwkv_k_k[layer], rwkv_k_a[layer], rwkv_r_k[layer], rwkv_ln_w[layer],
                         rwkv_ln_b[layer], bsz, seq)
        y_d = s5_mixer(p_s5, s5_a_re[layer], s5_a_im[layer], s5_b_re[layer], s5_b_im[layer], s5_c_re[layer],
                       s5_c_im[layer], s5_log_dt[layer], s5_d[layer], s5_glu_w[layer], s5_glu_b[layer],
                       s5_out_norm[layer], bsz, seq)
        y_d_tm = y_d.reshape(seq, bsz * GROUP_WIDTH)
        mixed = (x2, y_a, y_b, y_c, y_d_tm, w_out[layer].astype(BF16))
        idx = layer // 2
        if layer % 2 == 0:
            x2 = outproj_ffn(*mixed, ln_ffn[layer], ffn_w_gate[idx].astype(BF16), ffn_w_up[idx].astype(BF16),
                             ffn_w_down[idx].astype(BF16), seq)
        else:
            x2, route = outproj_router(*mixed, ln_ffn[layer], moe_router_w[idx], moe_router_b[idx], seq)
            x2 = moe_ffn(x2, route, ln_ffn[layer], moe_w_gate[idx].astype(BF16), moe_w_up[idx].astype(BF16),
                         moe_w_down[idx].astype(BF16))
    return x2.reshape(bsz, seq, dm)
```

```python
import functools
import math

import jax
import jax.numpy as jnp
from jax import lax
from jax.experimental import pallas as pl
from jax.experimental.pallas import tpu as pltpu

F32 = jnp.float32
BF16 = jnp.bfloat16

D_MODEL = 1024
GROUP_WIDTH = 256
HEAD_DIM = 64
NORM_EPS = 1e-6
D_FF = 2816

SSD_XBC = 512
SSD_SEG = 1024
ATT_SEG = 768
RWKV_SEG = 1024
S5_SEG = 256
SEG_WIDTHS = (SSD_SEG, ATT_SEG, RWKV_SEG, S5_SEG)
SEG_TOTAL = sum(SEG_WIDTHS)

VMEM_LIMIT = 56 * 1024 * 1024


def _params(sem):
    return pltpu.CompilerParams(dimension_semantics=sem, vmem_limit_bytes=VMEM_LIMIT)


def _rms(x, gain):
    ms = jnp.mean(x * x, axis=-1, keepdims=True)
    return x * lax.rsqrt(ms + NORM_EPS) * gain


def _inproj_body(x_ref, g_ref, w_ref, o_ssd, o_att, o_rwkv, o_s5):
    h = _rms(x_ref[...], g_ref[...]).astype(BF16)
    off = 0
    for o_ref in (o_ssd, o_att, o_rwkv, o_s5):
        n = o_ref.shape[-1]
        o_ref[...] = jnp.dot(h, w_ref[:, off:off + n], preferred_element_type=F32)
        off += n


def inproj(x2, gain, w_seg, bsz, seq, tm=512):
    n_tok = x2.shape[0]
    tiles_per_seq = seq // tm
    row = lambda i: (i, 0)
    return pl.pallas_call(
        _inproj_body,
        grid=(n_tok // tm,),
        in_specs=[pl.BlockSpec((tm, D_MODEL), row),
                  pl.BlockSpec((1, D_MODEL), lambda i: (0, 0)),
                  pl.BlockSpec((D_MODEL, SEG_TOTAL), lambda i: (0, 0))],
        out_specs=[pl.BlockSpec((tm, SSD_SEG), row),
                   pl.BlockSpec((tm, ATT_SEG), row),
                   pl.BlockSpec((tm, RWKV_SEG), row),
                   pl.BlockSpec((tm, S5_SEG), lambda i: (i % tiles_per_seq, i // tiles_per_seq))],
        out_shape=[jax.ShapeDtypeStruct((n_tok, SSD_SEG), F32),
                   jax.ShapeDtypeStruct((n_tok, ATT_SEG), F32),
                   jax.ShapeDtypeStruct((n_tok, RWKV_SEG), F32),
                   jax.ShapeDtypeStruct((seq, bsz * S5_SEG), F32)],
        compiler_params=_params(("parallel",)),
        name="inproj",
    )(x2, gain.reshape(1, D_MODEL), w_seg)


def _segment_w_in(w):
    dt_cols = jnp.repeat(w[:, 768:772], HEAD_DIM, axis=1)
    return jnp.concatenate([w[:, :768], dt_cols, w[:, 772:]], axis=1).astype(BF16)


def _mix_out(x_ref, y_refs, wo_ref):
    acc = x_ref[...]
    for j, y_ref in enumerate(y_refs):
        acc = acc + jnp.dot(y_ref[...].astype(BF16), wo_ref[j * GROUP_WIDTH:(j + 1) * GROUP_WIDTH, :],
                            preferred_element_type=F32)
    return acc


def _mix_out_specs(tm, seq):
    tiles_per_seq = seq // tm
    row = lambda i: (i, 0)
    return [pl.BlockSpec((tm, D_MODEL), row),
            pl.BlockSpec((tm, GROUP_WIDTH), row),
            pl.BlockSpec((tm, GROUP_WIDTH), row),
            pl.BlockSpec((tm, GROUP_WIDTH), row),
            pl.BlockSpec((tm, GROUP_WIDTH), lambda i: (i % tiles_per_seq, i // tiles_per_seq))]


def _resident(shape):
    return pl.BlockSpec(shape, lambda i: (0,) * len(shape), pipeline_mode=pl.Buffered(1))


FFN_COLS = 256


def _outproj_ffn_body(x_ref, ya_ref, yb_ref, yc_ref, yd_ref, wo_ref, g_ref, wg_ref, wu_ref, wd_ref, o_ref,
                      xn_scr, act_scr):
    xn = _mix_out(x_ref, (ya_ref, yb_ref, yc_ref, yd_ref), wo_ref)
    xn_scr[...] = xn
    h = _rms(xn, g_ref[...]).astype(BF16)
    for j in range(D_FF // FFN_COLS):
        cols = slice(j * FFN_COLS, (j + 1) * FFN_COLS)
        gate = jnp.dot(h, wg_ref[:, cols], preferred_element_type=F32)
        up = jnp.dot(h, wu_ref[:, cols], preferred_element_type=F32)
        act_scr[:, cols] = (gate * jax.nn.sigmoid(gate) * up).astype(BF16)
    act = act_scr[...]
    for n in range(D_MODEL // FFN_COLS):
        cols = slice(n * FFN_COLS, (n + 1) * FFN_COLS)
        o_ref[:, cols] = xn_scr[:, cols] + jnp.dot(act, wd_ref[:, cols], preferred_element_type=F32)


def outproj_ffn(x2, ya, yb, yc, yd_tm, w_out_bf, gain, wg, wu, wd, seq, tm=512):
    n_tok = x2.shape[0]
    return pl.pallas_call(
        _outproj_ffn_body,
        grid=(n_tok // tm,),
        in_specs=_mix_out_specs(tm, seq) + [
            _resident((D_MODEL, D_MODEL)), pl.BlockSpec((1, D_MODEL), lambda i: (0, 0)),
            _resident((D_MODEL, D_FF)), _resident((D_MODEL, D_FF)), _resident((D_FF, D_MODEL))],
        out_specs=pl.BlockSpec((tm, D_MODEL), lambda i: (i, 0)),
        out_shape=jax.ShapeDtypeStruct((n_tok, D_MODEL), F32),
        scratch_shapes=[pltpu.VMEM((tm, D_MODEL), F32), pltpu.VMEM((tm, D_FF), BF16)],
        compiler_params=_params(("parallel",)),
        name="outproj_ffn",
    )(x2, ya, yb, yc, yd_tm, w_out_bf, gain.reshape(1, D_MODEL), wg, wu, wd)


S5_GROUPS = 16
S5_GROUP = 16
S5_STATE = 64
S5_CH = S5_GROUPS * S5_STATE
S5_BT = 8
S5_TIME_SEG = 32


def _s5_body(u_ref, wb_ref, wc_ref, are_ref, aim_ref, d_ref, gw_ref, gb_ref, gain_ref, o_ref, h_scr, st_scr):
    tt = u_ref.shape[0]

    @pl.when(pl.program_id(1) == 0)
    def _():
        st_scr[...] = jnp.zeros_like(st_scr)

    a_re = jnp.broadcast_to(are_ref[...], (S5_BT, S5_CH))
    a_im = jnp.broadcast_to(aim_ref[...], (S5_BT, S5_CH))
    seg = S5_TIME_SEG
    n_seg = tt // seg

    def inputs(q):
        return u_ref[q * seg:(q + 1) * seg].reshape(seg * S5_BT, GROUP_WIDTH)

    def project_in(q):
        bu = jnp.dot(inputs(q).astype(BF16), wb_ref[...], preferred_element_type=F32)
        h_scr[q * seg:(q + 1) * seg] = bu.reshape(seg, S5_BT, 2 * S5_CH)

    def scan(q, h_re, h_im):
        for t in range(q * seg, (q + 1) * seg):
            n_re = a_re * h_re - a_im * h_im + h_scr[t, :, :S5_CH]
            n_im = a_re * h_im + a_im * h_re + h_scr[t, :, S5_CH:]
            h_scr[t, :, :S5_CH] = n_re
            h_scr[t, :, S5_CH:] = n_im
            h_re, h_im = n_re, n_im
        return h_re, h_im

    def project_out(q):
        hall = h_scr[q * seg:(q + 1) * seg].reshape(seg * S5_BT, 2 * S5_CH)
        y = jnp.dot(hall.astype(BF16), wc_ref[...], preferred_element_type=F32) + d_ref[...] * inputs(q)
        hh = jax.nn.gelu(y)
        gl = jnp.dot(hh.astype(BF16), gw_ref[...], preferred_element_type=F32) + gb_ref[...]
        out = _rms(hh * jax.nn.sigmoid(gl), gain_ref[...])
        o_ref[q * seg:(q + 1) * seg] = out.reshape(seg, S5_BT, GROUP_WIDTH)

    h_re, h_im = st_scr[0], st_scr[1]
    project_in(0)
    for q in range(n_seg):
        if q + 1 < n_seg:
            project_in(q + 1)
        h_re, h_im = scan(q, h_re, h_im)
        project_out(q)
    st_scr[0] = h_re
    st_scr[1] = h_im


def _s5_weights(a_re, a_im, b_re, b_im, c_re, c_im, log_dt):
    dt = jnp.exp(log_dt)[:, None]
    mag = jnp.exp(dt * a_re)
    ab_re = mag * jnp.cos(dt * a_im)
    ab_im = mag * jnp.sin(dt * a_im)
    den = a_re * a_re + a_im * a_im
    f_re = ((ab_re - 1.0) * a_re + ab_im * a_im) / den
    f_im = (ab_im * a_re - (ab_re - 1.0) * a_im) / den
    bb_re = f_re[..., None] * b_re - f_im[..., None] * b_im
    bb_im = f_re[..., None] * b_im + f_im[..., None] * b_re
    eye = jnp.eye(S5_GROUPS, dtype=F32)
    expand_b = lambda m: jnp.einsum('gpc,gh->gchp', m, eye).reshape(GROUP_WIDTH, S5_CH)
    expand_c = lambda m: jnp.einsum('gcp,gh->gphc', m, eye).reshape(S5_CH, GROUP_WIDTH)
    wb = jnp.concatenate([expand_b(bb_re), expand_b(bb_im)], axis=1).astype(BF16)
    wc = jnp.concatenate([expand_c(c_re), -expand_c(c_im)], axis=0).astype(BF16)
    return wb, wc, ab_re.reshape(1, S5_CH), ab_im.reshape(1, S5_CH)


def s5_mixer(u_tm, a_re, a_im, b_re, b_im, c_re, c_im, log_dt, d_skip, glu_w, glu_b, out_gain, bsz, seq, tt=128):
    wb, wc, are, aim = _s5_weights(a_re, a_im, b_re, b_im, c_re, c_im, log_dt)
    u3 = u_tm.reshape(seq, bsz, GROUP_WIDTH)
    const = lambda shape: pl.BlockSpec(shape, lambda i, j: (0,) * len(shape))
    row = lambda v: v.reshape(1, GROUP_WIDTH)
    return pl.pallas_call(
        _s5_body,
        grid=(bsz // S5_BT, seq // tt),
        in_specs=[pl.BlockSpec((tt, S5_BT, GROUP_WIDTH), lambda i, j: (j, i, 0)),
                  const((GROUP_WIDTH, 2 * S5_CH)), const((2 * S5_CH, GROUP_WIDTH)),
                  const((1, S5_CH)), const((1, S5_CH)), const((1, GROUP_WIDTH)),
                  const((GROUP_WIDTH, GROUP_WIDTH)), const((1, GROUP_WIDTH)), const((1, GROUP_WIDTH))],
        out_specs=pl.BlockSpec((tt, S5_BT, GROUP_WIDTH), lambda i, j: (j, i, 0)),
        out_shape=jax.ShapeDtypeStruct((seq, bsz, GROUP_WIDTH), F32),
        scratch_shapes=[pltpu.VMEM((tt, S5_BT, 2 * S5_CH), F32), pltpu.VMEM((2, S5_BT, S5_CH), F32)],
        compiler_params=_params(("parallel", "arbitrary")),
        name="s5_mixer",
    )(u3, wb, wc, are, aim, row(d_skip), glu_w.astype(BF16), row(glu_b), row(out_gain))


SSD_CHUNK = 128
SSD_CONV = 4
SSD_HEADS = 4
SSD_TAIL = 8
HIGHEST = lax.Precision.HIGHEST


def _dot_nt(a, b, **kw):
    return lax.dot_general(a, b, (((1,), (1,)), ((), ())), preferred_element_type=F32, **kw)


def _dot_tn(a, b, **kw):
    return lax.dot_general(a, b, (((0,), (0,)), ((), ())), preferred_element_type=F32, **kw)


def _ssd_body(p_ref, cw_ref, cb_ref, dtb_ref, arow_ref, drow_ref, gain_ref, o_ref, tail_scr, st_scr):
    tb = p_ref.shape[0]
    L = SSD_CHUNK

    @pl.when(pl.program_id(1) == 0)
    def _():
        tail_scr[...] = jnp.zeros_like(tail_scr)
        st_scr[...] = jnp.zeros_like(st_scr)

    raw = p_ref[:, :SSD_XBC]
    ext = jnp.concatenate([tail_scr[...], raw], axis=0)
    conv = cb_ref[...] + cw_ref[SSD_CONV - 1:SSD_CONV, :] * raw
    for j in range(SSD_CONV - 1):
        shifted = pltpu.roll(ext, SSD_CONV - 1 - j, axis=0)[SSD_TAIL:, :]
        conv = conv + cw_ref[j:j + 1, :] * shifted
    tail_scr[...] = raw[tb - SSD_TAIL:, :]
    xc = conv * jax.nn.sigmoid(conv)

    dt = jax.nn.softplus(p_ref[:, 768:1024] + dtb_ref[...])
    a = dt * arow_ref[...]

    ii = lax.broadcasted_iota(jnp.int32, (L, L), 0)
    jj = lax.broadcasted_iota(jnp.int32, (L, L), 1)
    causal = ii >= jj
    tril = causal.astype(BF16)
    low_half = jj < HEAD_DIM
    lane2 = lax.broadcasted_iota(jnp.int32, (L, GROUP_WIDTH), 1)
    lane1 = lax.broadcasted_iota(jnp.int32, (L, L), 1)
    row_grp = lax.broadcasted_iota(jnp.int32, (L, GROUP_WIDTH), 0) // HEAD_DIM
    state_mask = row_grp == lane2 // (2 * HEAD_DIM)

    for c in range(tb // L):
        sl = slice(c * L, (c + 1) * L)
        xs = xc[sl, :GROUP_WIDTH]
        bm = xc[sl, GROUP_WIDTH:GROUP_WIDTH + L]
        cm = xc[sl, GROUP_WIDTH + L:]
        xdt = xs * dt[sl]
        a_hi = a[sl].astype(BF16)
        a_lo = (a[sl] - a_hi.astype(F32)).astype(BF16)
        cs = (jnp.dot(tril, a_hi, preferred_element_type=F32)
              + jnp.dot(tril, a_lo, preferred_element_type=F32))
        cs_end = cs[L - 1:L, :]
        st = st_scr[...]
        y = jnp.exp(cs) * jnp.dot(cm.astype(BF16), st.astype(BF16), preferred_element_type=F32)
        for g in range(2):
            cg = jnp.where((lane1 // HEAD_DIM) == g, cm, 0.0)
            gram = _dot_nt(cg.astype(BF16), bm.astype(BF16))
            pair = cs[:, g * L:(g + 1) * L]
            swapped = pltpu.roll(pair, HEAD_DIM, axis=1)
            for k in range(2):
                h = 2 * g + k
                col = jnp.where(low_half, pair, swapped) if k == 0 else jnp.where(low_half, swapped, pair)
                seg = col - col.T
                decay = jnp.where(causal, jnp.exp(jnp.minimum(seg, 0.0)), 0.0)
                yd = jnp.dot((gram * decay).astype(BF16), xdt.astype(BF16), preferred_element_type=F32)
                y = y + jnp.where((lane2 // HEAD_DIM) == h, yd, 0.0)
        upd = _dot_tn(bm.astype(BF16), (xdt * jnp.exp(cs_end - cs)).astype(BF16))
        st_scr[...] = st * jnp.exp(cs_end) + jnp.where(state_mask, upd, 0.0)
        y = y + drow_ref[...] * xs
        z = p_ref[sl, 512:768]
        o_ref[sl, :] = _rms(y * (z * jax.nn.sigmoid(z)), gain_ref[...])


def ssd_mixer(p_ssd, conv_w, conv_b, dt_bias, a_log, d_skip, norm_gain, bsz, seq, tb=512):
    n_tok = p_ssd.shape[0]
    blocks = seq // tb
    per_head = lambda v: jnp.repeat(v.astype(F32), HEAD_DIM).reshape(1, GROUP_WIDTH)
    const = lambda shape: pl.BlockSpec(shape, lambda b, j: (0,) * len(shape))
    return pl.pallas_call(
        _ssd_body,
        grid=(bsz, blocks),
        in_specs=[pl.BlockSpec((tb, SSD_SEG), lambda b, j: (b * blocks + j, 0)),
                  const((SSD_CONV, SSD_XBC)), const((1, SSD_XBC)), const((1, GROUP_WIDTH)),
                  const((1, GROUP_WIDTH)), const((1, GROUP_WIDTH)), const((1, GROUP_WIDTH))],
        out_specs=pl.BlockSpec((tb, GROUP_WIDTH), lambda b, j: (b * blocks + j, 0)),
        out_shape=jax.ShapeDtypeStruct((n_tok, GROUP_WIDTH), F32),
        scratch_shapes=[pltpu.VMEM((SSD_TAIL, SSD_XBC), F32), pltpu.VMEM((2 * HEAD_DIM, GROUP_WIDTH), F32)],
        compiler_params=_params(("parallel", "arbitrary")),
        name="ssd_mixer",
    )(p_ssd, conv_w, conv_b.reshape(1, SSD_XBC), per_head(dt_bias), per_head(-jnp.exp(a_log)), per_head(d_skip),
      norm_gain.reshape(1, GROUP_WIDTH))


ATT_HEADS = 4
ATT_BLOCK = 128
DILATED_PATTERNS = ((128, 1), (512, 4), (2048, 16))
MASKED = -1e30
ATT_PAR = 4


def _att_bias():
    slopes = [2.0 ** (-8.0 / ATT_HEADS * (h + 1)) for h in range(ATT_HEADS)]
    q_idx = jnp.arange(ATT_BLOCK)[:, None]
    k_idx = jnp.arange(2 * ATT_BLOCK)[None, :] - ATT_BLOCK
    rel = q_idx - k_idx
    out = []
    for window, dilation in DILATED_PATTERNS:
        span = window // dilation
        valid = (rel >= 0) & (rel <= span)
        per_head = [jnp.where(valid, -s * (rel * dilation).astype(F32), MASKED) for s in slopes]
        out.append(jnp.stack(per_head))
    return jnp.stack(out)


def _ld2(scr, rows):
    return jnp.concatenate([scr[0, rows, :], scr[1, rows, :]], axis=1)


def _st2(scr, rows, val):
    scr[0, rows, :] = val[:, :ATT_BLOCK]
    scr[1, rows, :] = val[:, ATT_BLOCK:]


def _att_body(qkv_ref, gmat_ref, qg_ref, kg_ref, og_ref, bias_ref, o_ref, q_scr, k_scr, v_scr, u_scr, m_scr, s_scr):
    seq = qkv_ref.shape[0]
    blk = ATT_BLOCK
    lane = lax.broadcasted_iota(jnp.int32, (blk, GROUP_WIDTH), 1) // HEAD_DIM
    every = pl.ds(0, seq)

    def head_norm(t, gain):
        sq = t * t
        hi = sq.astype(BF16)
        lo = (sq - hi.astype(F32)).astype(BF16)
        ms = (jnp.dot(hi, gmat_ref[...], preferred_element_type=F32)
              + jnp.dot(lo, gmat_ref[...], preferred_element_type=F32))
        return t * lax.rsqrt(ms + NORM_EPS) * gain

    _st2(q_scr, every, head_norm(qkv_ref[:, :GROUP_WIDTH], qg_ref[...]) * (1.0 / math.sqrt(HEAD_DIM)))
    _st2(k_scr, every, head_norm(qkv_ref[:, GROUP_WIDTH:2 * GROUP_WIDTH], kg_ref[...]))
    _st2(v_scr, every, qkv_ref[:, 2 * GROUP_WIDTH:])

    prev_cols = lax.broadcasted_iota(jnp.int32, (1, 2 * blk), 1) < blk
    heads = range(ATT_HEADS)

    def blocks(pat, specs):
        pairs = [(b, h) for b in range(len(specs)) for h in heads]
        q, keys, vals, gate = [], [], [], []
        for rows, prev_rows, has_prev in specs:
            q.append(_ld2(q_scr, rows))
            if prev_rows is None:
                keys.append(_ld2(k_scr, rows).astype(BF16))
                vals.append(_ld2(v_scr, rows).astype(BF16))
                gate.append(None)
            else:
                keys.append(jnp.concatenate([_ld2(k_scr, prev_rows), _ld2(k_scr, rows)], axis=0).astype(BF16))
                vals.append(jnp.concatenate([_ld2(v_scr, prev_rows), _ld2(v_scr, rows)], axis=0).astype(BF16))
                gate.append(jnp.where(jnp.logical_or(has_prev, jnp.logical_not(prev_cols)), 0.0, MASKED))
        bias = {(b, h): (bias_ref[pat, h, :, blk:] if gate[b] is None else bias_ref[pat, h] + gate[b])
                for b, h in pairs}
        qh = {(b, h): jnp.where(lane == h, q[b], 0.0).astype(BF16) for b, h in pairs}
        sc = {(b, h): _dot_nt(qh[b, h], keys[b]) + bias[b, h] for b, h in pairs}
        m = {bh: jnp.max(sc[bh], axis=-1, keepdims=True) for bh in pairs}
        p = {bh: jnp.exp(sc[bh] - m[bh]) for bh in pairs}
        den = {bh: jnp.sum(p[bh], axis=-1, keepdims=True) for bh in pairs}
        u = {(b, h): jnp.dot(p[b, h].astype(BF16), vals[b], preferred_element_type=F32) for b, h in pairs}
        out = []
        for b in range(len(specs)):
            u_all, m_all, s_all = u[b, 0], m[b, 0], den[b, 0]
            for h in heads[1:]:
                sel = lane == h
                u_all = jnp.where(sel, u[b, h], u_all)
                m_all = jnp.where(sel, m[b, h], m_all)
                s_all = jnp.where(sel, den[b, h], s_all)
            out.append((u_all, m_all, s_all))
        return out

    def merge(rows, u, m, s):
        m0 = _ld2(m_scr, rows)
        m_new = jnp.maximum(m0, m)
        alpha = jnp.exp(m0 - m_new)
        beta = jnp.exp(m - m_new)
        return alpha * _ld2(u_scr, rows) + beta * u, m_new, alpha * _ld2(s_scr, rows) + beta * s

    def save(rows, u, m, s):
        _st2(u_scr, rows, u)
        _st2(m_scr, rows, m)
        _st2(s_scr, rows, s)

    par = ATT_PAR
    n_iter = 16 // par

    def body16(i, _):
        specs = [(pl.ds(i + n_iter * b, blk, stride=16), None, None) for b in range(par)]
        for spec, res in zip(specs, blocks(2, specs)):
            save(spec[0], *res)
        return 0
    lax.fori_loop(0, n_iter, body16, 0)

    def body4(i, _):
        specs = []
        for b in range(par):
            idx = i + n_iter * b
            r = idx % 4
            n = idx // 4
            specs.append((pl.ds(n * (4 * blk) + r, blk, stride=4),
                          pl.ds(jnp.maximum(n - 1, 0) * (4 * blk) + r, blk, stride=4), n > 0))
        for spec, res in zip(specs, blocks(1, specs)):
            save(spec[0], *merge(spec[0], *res))
        return 0
    lax.fori_loop(0, n_iter, body4, 0)

    def body1(i, _):
        specs = []
        for b in range(par):
            n = i + (seq // blk // par) * b
            specs.append((pl.ds(pl.multiple_of(n * blk, blk), blk),
                          pl.ds(pl.multiple_of(jnp.maximum(n - 1, 0) * blk, blk), blk), n > 0))
        for spec, res in zip(specs, blocks(0, specs)):
            u, m, s = merge(spec[0], *res)
            o_ref[spec[0], :] = _rms(u / s, og_ref[...])
        return 0
    lax.fori_loop(0, seq // blk // par, body1, 0)


def attention_mixer(qkv, q_gain, k_gain, out_gain, bsz, seq):
    n_tok = qkv.shape[0]
    head_id = jnp.arange(GROUP_WIDTH) // HEAD_DIM
    gmat = ((head_id[:, None] == head_id[None, :]).astype(F32) / HEAD_DIM).astype(BF16)
    tile = lambda g: jnp.tile(g.astype(F32), ATT_HEADS).reshape(1, GROUP_WIDTH)
    const = lambda shape: pl.BlockSpec(shape, lambda b: (0,) * len(shape))
    return pl.pallas_call(
        _att_body,
        grid=(bsz,),
        in_specs=[pl.BlockSpec((seq, ATT_SEG), lambda b: (b, 0)),
                  const((GROUP_WIDTH, GROUP_WIDTH)), const((1, GROUP_WIDTH)), const((1, GROUP_WIDTH)),
                  const((1, GROUP_WIDTH)), const((3, ATT_HEADS, ATT_BLOCK, 2 * ATT_BLOCK))],
        out_specs=pl.BlockSpec((seq, GROUP_WIDTH), lambda b: (b, 0)),
        out_shape=jax.ShapeDtypeStruct((n_tok, GROUP_WIDTH), F32),
        scratch_shapes=[pltpu.VMEM((2, seq, ATT_BLOCK), F32) for _ in range(6)],
        compiler_params=_params(("parallel",)),
        name="attention_mixer",
    )(qkv, gmat, tile(q_gain), tile(k_gain), out_gain.reshape(1, GROUP_WIDTH), _att_bias())


RWKV_HEADS = 4
RWKV_CHUNK = 64
RWKV_DECAY_SCALE = 0.606531
RWKV_GN_EPS = 64e-5
RWKV_MAT = RWKV_HEADS * RWKV_CHUNK


def _per_head_blocks(x):
    lane_head = lax.broadcasted_iota(jnp.int32, x.shape, 1) // HEAD_DIM
    return jnp.concatenate([jnp.where(lane_head == h, x, 0.0) for h in range(RWKV_HEADS)], axis=0)


def _rwkv_body(p_ref, mu_ref, w0_ref, w2_ref, a0_ref, a2_ref, g2_ref, kk_ref, ka_ref, rk_ref, lnw_ref, lnb_ref,
               hsum_ref, o_ref, prev_scr, z_scr):
    nb, tb = p_ref.shape[0], p_ref.shape[1]
    L = RWKV_CHUNK

    def mb(a, b):
        return jnp.dot(a.astype(BF16), b.astype(BF16), preferred_element_type=F32)

    def split(x):
        hi = x.astype(BF16)
        return hi, (x - hi.astype(F32)).astype(BF16)

    def head_sum(x):
        return jnp.dot(x.astype(BF16), hsum_ref[...], preferred_element_type=F32)

    @pl.when(pl.program_id(1) == 0)
    def _():
        prev_scr[...] = jnp.zeros_like(prev_scr)
        z_scr[...] = jnp.zeros_like(z_scr)

    ii = lax.broadcasted_iota(jnp.int32, (L, RWKV_MAT), 0)
    jj = lax.broadcasted_iota(jnp.int32, (L, RWKV_MAT), 1) % L
    eye_all = (ii == jj).astype(F32)
    ti = lax.broadcasted_iota(jnp.int32, (L, L), 0)
    tj = lax.broadcasted_iota(jnp.int32, (L, L), 1)
    tril = (ti >= tj).astype(BF16)
    first_row = lax.broadcasted_iota(jnp.int32, (L, RWKV_SEG), 0) == 0
    zi = lax.broadcasted_iota(jnp.int32, (RWKV_MAT, RWKV_MAT), 0) // HEAD_DIM
    zj = lax.broadcasted_iota(jnp.int32, (RWKV_MAT, RWKV_MAT), 1) // HEAD_DIM
    same_head = zi == zj

    def each(fn, *lists):
        return [fn(*args) for args in zip(*lists)]

    def chunks(c, _):
        seqs = list(range(nb))
        rows = pl.ds(pl.multiple_of(c * L, L), L)
        pc = [p_ref[s, rows, :] for s in seqs]
        prev = [jnp.where(first_row, prev_scr[s], pltpu.roll(pc[s], 1, axis=0)) for s in seqs]
        for s in seqs:
            prev_scr[s] = pc[s][L - 1:L, :]
        pf = each(lambda x, xp: x + mu_ref[...] * (xp - x), pc, prev)
        r = [x[:, :GROUP_WIDTH] for x in pf]
        k = [x[:, GROUP_WIDTH:2 * GROUP_WIDTH] for x in pf]
        v = [x[:, 2 * GROUP_WIDTH:3 * GROUP_WIDTH] for x in pf]
        wa = [x[:, 3 * GROUP_WIDTH:3 * GROUP_WIDTH + 128] for x in pf]
        gd = [x[:, 3 * GROUP_WIDTH + 128:] for x in pf]
        log_w = each(lambda x: -RWKV_DECAY_SCALE * jax.nn.sigmoid(
            w0_ref[...] + jnp.dot(jnp.tanh(x).astype(BF16), w2_ref[...], preferred_element_type=F32)), wa)
        alr = each(lambda x: jax.nn.sigmoid(
            a0_ref[...] + jnp.dot(x.astype(BF16), a2_ref[...], preferred_element_type=F32)), wa)
        g = each(lambda x: jnp.dot(jax.nn.sigmoid(x).astype(BF16), g2_ref[...], preferred_element_type=F32), gd)
        kk = each(lambda x: x * kk_ref[...], k)
        kk_ss = each(lambda x: head_sum(x * x), kk)
        kk = each(lambda x, ss: x * lax.rsqrt(ss + 1e-12), kk, kk_ss)
        k2 = each(lambda x, a: x * (1.0 + (a - 1.0) * ka_ref[...]), k, alr)

        def cumsum(lw):
            hi, lo = split(lw)
            return (jnp.dot(tril, hi, preferred_element_type=F32) + jnp.dot(tril, lo, preferred_element_type=F32))
        cs = each(cumsum, log_w)
        gam = each(jnp.exp, cs)
        gam_inv = each(lambda x: jnp.exp(-x), cs)
        a_hat = each(lambda x, c_, lw: -x * jnp.exp(c_ - lw), kk, cs, log_w)
        b_hat = each(lambda x, a, gi: x * a * gi, kk, alr, gam_inv)
        k_hat = each(lambda x, gi: x * gi, k2, gam_inv)
        r_hat = each(lambda x, gm: x * gm, r, gam)

        gram = each(lambda a, rr, b, kh: _dot_nt(
            jnp.concatenate([a, rr], axis=0).astype(BF16),
            jnp.concatenate([_per_head_blocks(b), _per_head_blocks(kh)], axis=0).astype(BF16)),
            a_hat, r_hat, b_hat, k_hat)
        n_all = [jnp.where(ii > jj, x[:L, :RWKV_MAT], 0.0) for x in gram]
        m_all = [jnp.where(ii > jj, x[:L, RWKV_MAT:], 0.0) for x in gram]
        gb_all = [jnp.where(ii >= jj, x[L:, :RWKV_MAT], 0.0) for x in gram]
        gk_all = [jnp.where(ii >= jj, x[L:, RWKV_MAT:], 0.0) for x in gram]

        w_all = [eye_all + n for n in n_all]
        n_pow = each(lambda n: mb(n, _per_head_blocks(n)), n_all)
        for _ in range(4):
            res = each(lambda n, w: mb(jnp.concatenate([n, w], axis=0), _per_head_blocks(n)), n_pow, w_all)
            n_pow = [x[:L] for x in res]
            w_all = each(lambda w, x: w + x[L:], w_all, res)
        w_all = each(lambda w, n: w + mb(w, _per_head_blocks(n)), w_all, n_pow)

        t1 = each(lambda m, vv: mb(m, _per_head_blocks(vv)), m_all, v)
        at_vt = each(lambda w, a, t: mb(w, jnp.concatenate([_per_head_blocks(a), _per_head_blocks(t)], axis=1)),
                     w_all, a_hat, t1)
        a_til = [x[:, :RWKV_MAT] for x in at_vt]
        v_til = [x[:, RWKV_MAT:] for x in at_vt]

        z = [z_scr[s] for s in seqs]
        ax_rx = each(lambda a, rr, zz: _dot_nt(jnp.concatenate([a, rr], axis=0).astype(BF16), zz.astype(BF16)),
                     a_til, r_hat, z)
        u = each(lambda x, vt: x[:L] + vt, ax_rx, v_til)
        y = each(lambda x, gb, gk, uu, vv: x[L:] + mb(
            jnp.concatenate([gb, gk], axis=1),
            jnp.concatenate([_per_head_blocks(uu), _per_head_blocks(vv)], axis=0)), ax_rx, gb_all, gk_all, u, v)
        d = each(lambda uu, vv, b, kh: _dot_tn(jnp.concatenate([uu, vv], axis=0).astype(BF16),
                                               jnp.concatenate([b, kh], axis=0).astype(BF16)), u, v, b_hat, k_hat)
        for s in seqs:
            z_scr[s] = (z[s] + jnp.where(same_head, d[s], 0.0)) * gam[s][L - 1:L, :]

        mean = each(lambda x: head_sum(x) * (1.0 / HEAD_DIM), y)
        yc = each(lambda x, m: x - m, y, mean)
        var = each(lambda x: head_sum(x * x) * (1.0 / HEAD_DIM), yc)
        yn = each(lambda x, vr: x * lax.rsqrt(vr + RWKV_GN_EPS) * lnw_ref[...] + lnb_ref[...], yc, var)
        bonus = each(lambda rr, kx, vv: head_sum(rr * kx * rk_ref[...]) * vv, r, k2, v)
        for s in seqs:
            o_ref[s, rows, :] = (yn[s] + bonus[s]) * g[s]
        return 0

    lax.fori_loop(0, tb // L, chunks, 0)


def rwkv_mixer(p_rwkv, mu, w0, w2, a0, a2, g2, k_k, k_a, r_k, ln_w, ln_b, bsz, seq, tb=256, nb=8):
    n_tok = p_rwkv.shape[0]
    nb = nb if bsz % nb == 0 else 1
    row = lambda t: t.reshape(1, -1).astype(F32)
    w2p = jnp.concatenate([w2, jnp.zeros_like(a2)], axis=0).astype(BF16)
    a2p = jnp.concatenate([jnp.zeros_like(w2), a2], axis=0).astype(BF16)
    head_id = jnp.arange(GROUP_WIDTH) // HEAD_DIM
    hsum = (head_id[:, None] == head_id[None, :]).astype(BF16)
    const = lambda shape: pl.BlockSpec(shape, lambda b, j: (0,) * len(shape))
    vec = const((1, GROUP_WIDTH))
    return pl.pallas_call(
        _rwkv_body,
        grid=(bsz // nb, seq // tb),
        in_specs=[pl.BlockSpec((nb, tb, RWKV_SEG), lambda b, j: (b, j, 0)),
                  const((1, RWKV_SEG)), vec, const((128, GROUP_WIDTH)), vec, const((128, GROUP_WIDTH)),
                  const((128, GROUP_WIDTH)), vec, vec, vec, vec, vec, const((GROUP_WIDTH, GROUP_WIDTH))],
        out_specs=pl.BlockSpec((nb, tb, GROUP_WIDTH), lambda b, j: (b, j, 0)),
        out_shape=jax.ShapeDtypeStruct((bsz, seq, GROUP_WIDTH), F32),
        scratch_shapes=[pltpu.VMEM((nb, 1, RWKV_SEG), F32), pltpu.VMEM((nb, RWKV_MAT, RWKV_MAT), F32)],
        compiler_params=_params(("parallel", "arbitrary")),
        name="rwkv_mixer",
    )(p_rwkv.reshape(bsz, seq, RWKV_SEG), row(mu), row(w0), w2p, row(a0), a2p, g2.astype(BF16), row(k_k), row(k_a),
      row(r_k), row(ln_w), row(ln_b), hsum).reshape(n_tok, GROUP_WIDTH)


N_EXPERTS = 8
TOP_K = 2
MOE_ROWS = 256
ROUTE_LANES = 128
FF_CHUNK = 256
OUT_SLOTS = 3


def _outproj_router_body(x_ref, ya_ref, yb_ref, yc_ref, yd_ref, wo_ref, g_ref, w_ref, b_ref, xn_ref, o_ref):
    xn = _mix_out(x_ref, (ya_ref, yb_ref, yc_ref, yd_ref), wo_ref)
    xn_ref[...] = xn
    h = _rms(xn, g_ref[...])
    h_hi = h.astype(BF16)
    h_lo = (h - h_hi.astype(F32)).astype(BF16)
    both = jnp.dot(h_hi, w_ref[...], preferred_element_type=F32)
    logits = (both[:, :ROUTE_LANES] + both[:, ROUTE_LANES:]
              + jnp.dot(h_lo, w_ref[:, :ROUTE_LANES], preferred_element_type=F32) + b_ref[...])
    lane = lax.broadcasted_iota(jnp.int32, logits.shape, 1)
    m1 = jnp.max(logits, axis=-1, keepdims=True)
    i1 = jnp.min(jnp.where(logits == m1, lane, ROUTE_LANES), axis=-1, keepdims=True)
    rest = jnp.where(lane == i1, MASKED, logits)
    m2 = jnp.max(rest, axis=-1, keepdims=True)
    i2 = jnp.min(jnp.where(rest == m2, lane, ROUTE_LANES), axis=-1, keepdims=True)
    e2 = jnp.exp(m2 - m1)
    g1 = 1.0 / (1.0 + e2)
    out = jnp.where(lane == 0, i1.astype(F32), jnp.where(lane == 1, i2.astype(F32),
                    jnp.where(lane == 2, g1, jnp.where(lane == 3, e2 * g1, 0.0))))
    o_ref[...] = out


def outproj_router(x2, ya, yb, yc, yd_tm, w_out_bf, gain, router_w, router_b, seq, tm=512):
    n_tok = x2.shape[0]
    w = jnp.pad(router_w.astype(F32), ((0, 0), (0, ROUTE_LANES - N_EXPERTS)))
    w_hi = w.astype(BF16)
    w = jnp.concatenate([w_hi, (w - w_hi.astype(F32)).astype(BF16)], axis=1)
    b = jnp.pad(router_b.astype(F32), (0, ROUTE_LANES - N_EXPERTS), constant_values=MASKED).reshape(1, ROUTE_LANES)
    return pl.pallas_call(
        _outproj_router_body,
        grid=(n_tok // tm,),
        in_specs=_mix_out_specs(tm, seq) + [
            _resident((D_MODEL, D_MODEL)), pl.BlockSpec((1, D_MODEL), lambda i: (0, 0)),
            pl.BlockSpec((D_MODEL, 2 * ROUTE_LANES), lambda i: (0, 0)),
            pl.BlockSpec((1, ROUTE_LANES), lambda i: (0, 0))],
        out_specs=[pl.BlockSpec((tm, D_MODEL), lambda i: (i, 0)),
                   pl.BlockSpec((tm, ROUTE_LANES), lambda i: (i, 0))],
        out_shape=[jax.ShapeDtypeStruct((n_tok, D_MODEL), F32),
                   jax.ShapeDtypeStruct((n_tok, ROUTE_LANES), F32)],
        compiler_params=_params(("parallel",)),
        name="outproj_router",
    )(x2, ya, yb, yc, yd_tm, w_out_bf, gain.reshape(1, D_MODEL), w, b)


def _moe_dispatch(experts, n_tok):
    n_assign = n_tok * TOP_K
    n_blocks = n_assign // MOE_ROWS + N_EXPERTS
    flat_exp = experts.reshape(-1)
    order = jnp.argsort(flat_exp, stable=True).astype(jnp.int32)
    counts = jnp.sum((flat_exp[:, None] == jnp.arange(N_EXPERTS, dtype=jnp.int32)[None, :]).astype(jnp.int32), axis=0)
    start = jnp.cumsum(counts) - counts
    padded = (counts + MOE_ROWS - 1) // MOE_ROWS * MOE_ROWS
    pad_end = jnp.cumsum(padded)
    pad_start = pad_end - padded
    block_first = jnp.arange(n_blocks, dtype=jnp.int32) * MOE_ROWS
    block_exp = jnp.minimum(jnp.searchsorted(pad_end, block_first, side='right'), N_EXPERTS - 1).astype(jnp.int32)
    first_rank = block_first - pad_start[block_exp]
    row = jnp.arange(MOE_ROWS, dtype=jnp.int32)[None, :]
    valid = row < (counts[block_exp] - first_rank)[:, None]
    src = jnp.clip((start[block_exp] + first_rank)[:, None] + row, 0, n_assign - 1)
    slot_asg = jnp.where(valid, order[src], -1)
    slot_tok = jnp.where(valid, slot_asg // TOP_K, 0)
    spare = (n_assign + jnp.cumsum((~valid).reshape(-1).astype(jnp.int32)) - 1).reshape(n_blocks, MOE_ROWS)
    slot_dst = jnp.where(valid, (slot_asg % TOP_K) * n_tok + slot_asg // TOP_K, spare)
    return (slot_tok.reshape(n_blocks, 1, MOE_ROWS), slot_dst.reshape(n_blocks, 1, MOE_ROWS), block_exp)


def _expert_body(bexp_ref, tok0_ref, tok_next_ref, dst_prev_ref, dst_cur_ref, x_hbm, g_ref, wg_ref, wu_ref, wd_ref,
                 y_hbm, xbuf, obuf, act_scr, sem_in, sem_out):
    del bexp_ref
    i = pl.program_id(0)
    last = pl.num_programs(0) - 1
    cur = i % 2
    nxt = 1 - cur
    ocur = i % OUT_SLOTS
    oprev = (i + OUT_SLOTS - 1) % OUT_SLOTS
    n_chunks = D_FF // FF_CHUNK

    def gather_row(tok_ref, r, slot):
        return pltpu.make_async_copy(x_hbm.at[pl.ds(tok_ref[0, 0, r], 1)], xbuf.at[slot, pl.ds(r, 1)], sem_in.at[slot])

    def scatter_row(dst_ref, r, slot):
        return pltpu.make_async_copy(obuf.at[slot, pl.ds(r, 1)], y_hbm.at[pl.ds(dst_ref[0, 0, r], 1)], sem_out.at[slot])

    def wait_gather(slot):
        pltpu.make_async_copy(x_hbm.at[pl.ds(0, MOE_ROWS)], xbuf.at[slot], sem_in.at[slot]).wait()

    def wait_scatter(slot):
        pltpu.make_async_copy(obuf.at[slot], y_hbm.at[pl.ds(0, MOE_ROWS)], sem_out.at[slot]).wait()

    @pl.when(i == 0)
    def _():
        def start(r, _):
            gather_row(tok0_ref, r, 0).start()
            return 0
        lax.fori_loop(0, MOE_ROWS, start, 0)
        obuf[OUT_SLOTS - 1] = jnp.zeros((MOE_ROWS, D_MODEL), F32)

    wait_gather(cur)

    @pl.when(i >= OUT_SLOTS - 1)
    def _():
        wait_scatter(ocur)

    h = _rms(xbuf[cur], g_ref[...]).astype(BF16)
    out_chunks = D_MODEL // FF_CHUNK
    stages = n_chunks + out_chunks
    gather_stages = stages // 2
    rows_per_gather = -(-MOE_ROWS // gather_stages)
    rows_per_scatter = -(-MOE_ROWS // (stages - gather_stages))

    def start_rows(stage):
        if stage < gather_stages:
            for r in range(stage * rows_per_gather, min((stage + 1) * rows_per_gather, MOE_ROWS)):
                gather_row(tok_next_ref, r, nxt).start()
        else:
            st = stage - gather_stages
            for r in range(st * rows_per_scatter, min((st + 1) * rows_per_scatter, MOE_ROWS)):
                scatter_row(dst_prev_ref, r, oprev).start()

    def zero_after_starts():
        probe = pltpu.bitcast(xbuf[cur, 0:8, 0:FF_CHUNK], jnp.uint32)
        return pltpu.bitcast(lax.shift_right_logical(probe, jnp.uint32(32)), F32)[0:1, :]

    for j in range(n_chunks):
        start_rows(j)
        cols = slice(j * FF_CHUNK, (j + 1) * FF_CHUNK)
        gate = jnp.dot(h, wg_ref[0, :, cols], preferred_element_type=F32)
        up = jnp.dot(h, wu_ref[0, :, cols], preferred_element_type=F32)
        if j < gather_stages:
            up = up + zero_after_starts()
        act_scr[:, cols] = (gate * jax.nn.sigmoid(gate) * up).astype(BF16)

    act = act_scr[...]
    for n in range(out_chunks):
        start_rows(n_chunks + n)
        cols = slice(n * FF_CHUNK, (n + 1) * FF_CHUNK)
        obuf[ocur, :, cols] = jnp.dot(act, wd_ref[0, :, cols], preferred_element_type=F32)

    @pl.when(i == last)
    def _():
        def start(r, _):
            scatter_row(dst_cur_ref, r, ocur).start()
            return 0
        lax.fori_loop(0, MOE_ROWS, start, 0)
        for slot in range(OUT_SLOTS):
            wait_scatter(slot)
        wait_gather(nxt)


def moe_experts(x2, gain, slot_tok, slot_dst, block_exp, wg, wu, wd):
    n_blocks = block_exp.shape[0]
    cap = n_blocks * MOE_ROWS
    spare = (cap + jnp.arange(MOE_ROWS, dtype=jnp.int32)).reshape(1, 1, MOE_ROWS)
    dst_ext = jnp.concatenate([spare, slot_dst], axis=0)
    smem = lambda index: pl.BlockSpec((1, 1, MOE_ROWS), index, memory_space=pltpu.SMEM)
    grid_spec = pltpu.PrefetchScalarGridSpec(
        num_scalar_prefetch=1,
        grid=(n_blocks,),
        in_specs=[smem(lambda i, be: (0, 0, 0)),
                  smem(lambda i, be: (jnp.minimum(i + 1, n_blocks - 1), 0, 0)),
                  smem(lambda i, be: (i, 0, 0)),
                  smem(lambda i, be: (i + 1, 0, 0)),
                  pl.BlockSpec(memory_space=pl.ANY),
                  pl.BlockSpec((1, D_MODEL), lambda i, be: (0, 0)),
                  pl.BlockSpec((1, D_MODEL, D_FF), lambda i, be: (be[i], 0, 0)),
                  pl.BlockSpec((1, D_MODEL, D_FF), lambda i, be: (be[i], 0, 0)),
                  pl.BlockSpec((1, D_FF, D_MODEL), lambda i, be: (be[i], 0, 0))],
        out_specs=pl.BlockSpec(memory_space=pl.ANY),
        scratch_shapes=[pltpu.VMEM((2, MOE_ROWS, D_MODEL), F32), pltpu.VMEM((OUT_SLOTS, MOE_ROWS, D_MODEL), F32),
                        pltpu.VMEM((MOE_ROWS, D_FF), BF16),
                        pltpu.SemaphoreType.DMA((2,)), pltpu.SemaphoreType.DMA((OUT_SLOTS,))],
    )
    return pl.pallas_call(
        _expert_body,
        grid_spec=grid_spec,
        out_shape=jax.ShapeDtypeStruct((cap + MOE_ROWS, D_MODEL), F32),
        compiler_params=_params(("arbitrary",)),
        name="moe_experts",
    )(block_exp, slot_tok, slot_tok, dst_ext, dst_ext, x2, gain.reshape(1, D_MODEL), wg, wu, wd)


def _combine_body(x_ref, y1_ref, y2_ref, r_ref, o_ref):
    route = r_ref[...]
    o_ref[...] = x_ref[...] + route[:, 2:3] * y1_ref[...] + route[:, 3:4] * y2_ref[...]


def moe_combine(x2, y, route, tm=512):
    n_tok = x2.shape[0]
    tiles = n_tok // tm
    return pl.pallas_call(
        _combine_body,
        grid=(tiles,),
        in_specs=[pl.BlockSpec((tm, D_MODEL), lambda i: (i, 0)),
                  pl.BlockSpec((tm, D_MODEL), lambda i: (i, 0)),
                  pl.BlockSpec((tm, D_MODEL), lambda i: (tiles + i, 0)),
                  pl.BlockSpec((tm, ROUTE_LANES), lambda i: (i, 0))],
        out_specs=pl.BlockSpec((tm, D_MODEL), lambda i: (i, 0)),
        out_shape=jax.ShapeDtypeStruct((n_tok, D_MODEL), F32),
        compiler_params=_params(("parallel",)),
        name="moe_combine",
    )(x2, y, y, route)


def moe_ffn(x2, route, gain, wg, wu, wd):
    n_tok = x2.shape[0]
    experts = route[:, :TOP_K].astype(jnp.int32)
    slot_tok, slot_dst, block_exp = _moe_dispatch(experts, n_tok)
    y = moe_experts(x2, gain, slot_tok, slot_dst, block_exp, wg, wu, wd)
    return moe_combine(x2, y, route)


def kernel(x, ln_mix, w_in, w_out, ssd_conv_w, ssd_conv_b, ssd_dt_bias, ssd_a_log, ssd_d, ssd_norm, att_q_norm, att_k_norm, att_out_norm, rwkv_mu, rwkv_w0, rwkv_w2, rwkv_a0, rwkv_a2, rwkv_g2, rwkv_k_k, rwkv_k_a, rwkv_r_k, rwkv_ln_w, rwkv_ln_b, s5_a_re, s5_a_im, s5_b_re, s5_b_im, s5_c_re, s5_c_im, s5_log_dt, s5_d, s5_glu_w, s5_glu_b, s5_out_norm, ln_ffn, ffn_w_gate, ffn_w_up, ffn_w_down, moe_router_w, moe_router_b, moe_w_gate, moe_w_up, moe_w_down):
    bsz, seq, dm = x.shape
    n_tok = bsz * seq
    depth = ln_mix.shape[0]
    x2 = x.reshape(n_tok, dm)
    for layer in range(depth):
        w_seg = _segment_w_in(w_in[layer])
        p_ssd, p_att, p_rwkv, p_s5 = inproj(x2, ln_mix[layer], w_seg, bsz, seq)
        y_a = ssd_mixer(p_ssd, ssd_conv_w[layer], ssd_conv_b[layer], ssd_dt_bias[layer], ssd_a_log[layer],
                        ssd_d[layer], ssd_norm[layer], bsz, seq)
        y_b = attention_mixer(p_att, att_q_norm[layer], att_k_norm[layer], att_out_norm[layer], bsz, seq)
        y_c = rwkv_mixer(p_rwkv, rwkv_mu[layer], rwkv_w0[layer], rwkv_w2[layer], rwkv_a0[layer], rwkv_a2[layer],
                         rwkv_g2[layer], rwkv_k_k[layer], rwkv_k_a[layer], rwkv_r_k[layer], rwkv_ln_w[layer],
                         rwkv_ln_b[layer], bsz, seq)
        y_d = s5_mixer(p_s5, s5_a_re[layer], s5_a_im[layer], s5_b_re[layer], s5_b_im[layer], s5_c_re[layer],
                       s5_c_im[layer], s5_log_dt[layer], s5_d[layer], s5_glu_w[layer], s5_glu_b[layer],
                       s5_out_norm[layer], bsz, seq)
        y_d_tm = y_d.reshape(seq, bsz * GROUP_WIDTH)
        mixed = (x2, y_a, y_b, y_c, y_d_tm, w_out[layer].astype(BF16))
        idx = layer // 2
        if layer % 2 == 0:
            x2 = outproj_ffn(*mixed, ln_ffn[layer], ffn_w_gate[idx].astype(BF16), ffn_w_up[idx].astype(BF16),
                             ffn_w_down[idx].astype(BF16), seq)
        else:
            x2, route = outproj_router(*mixed, ln_ffn[layer], moe_router_w[idx], moe_router_b[idx], seq)
            x2 = moe_ffn(x2, route, ln_ffn[layer], moe_w_gate[idx].astype(BF16), moe_w_up[idx].astype(BF16),
                         moe_w_down[idx].astype(BF16))
    return x2.reshape(bsz, seq, dm)
```

```python
import functools
import math

import jax
import jax.numpy as jnp
from jax import lax
from jax.experimental import pallas as pl
from jax.experimental.pallas import tpu as pltpu

F32 = jnp.float32
BF16 = jnp.bfloat16

D_MODEL = 1024
GROUP_WIDTH = 256
HEAD_DIM = 64
NORM_EPS = 1e-6
D_FF = 2816

SSD_XBC = 512
SSD_SEG = 1024
ATT_SEG = 768
RWKV_SEG = 1024
S5_SEG = 256
SEG_WIDTHS = (SSD_SEG, ATT_SEG, RWKV_SEG, S5_SEG)
SEG_TOTAL = sum(SEG_WIDTHS)

VMEM_LIMIT = 56 * 1024 * 1024


def _params(sem):
    return pltpu.CompilerParams(dimension_semantics=sem, vmem_limit_bytes=VMEM_LIMIT)


def _rms(x, gain):
    ms = jnp.mean(x * x, axis=-1, keepdims=True)
    return x * lax.rsqrt(ms + NORM_EPS) * gain


def _project_in(x, g_ref, w_ref, outs):
    h = _rms(x, g_ref[...]).astype(BF16)
    off = 0
    for o_ref in outs:
        n = o_ref.shape[-1]
        o_ref[...] = jnp.dot(h, w_ref[:, off:off + n], preferred_element_type=F32)
        off += n


def _inproj_body(x_ref, g_ref, w_ref, o_ssd, o_att, o_rwkv, o_s5):
    _project_in(x_ref[...], g_ref, w_ref, (o_ssd, o_att, o_rwkv, o_s5))


def _combine_inproj_body(x_ref, y1_ref, y2_ref, r_ref, g_ref, w_ref, o_x, o_ssd, o_att, o_rwkv, o_s5):
    route = r_ref[...]
    x = x_ref[...] + route[:, 2:3] * y1_ref[...] + route[:, 3:4] * y2_ref[...]
    o_x[...] = x
    _project_in(x, g_ref, w_ref, (o_ssd, o_att, o_rwkv, o_s5))


def inproj(x2, gain, w_seg, bsz, seq, pending=None, tm=512):
    n_tok = x2.shape[0]
    tiles = n_tok // tm
    tiles_per_seq = seq // tm
    row = lambda i: (i, 0)
    const = [pl.BlockSpec((1, D_MODEL), lambda i: (0, 0)), pl.BlockSpec((D_MODEL, SEG_TOTAL), lambda i: (0, 0))]
    out_specs = [pl.BlockSpec((tm, SSD_SEG), row),
                 pl.BlockSpec((tm, ATT_SEG), row),
                 pl.BlockSpec((tm, RWKV_SEG), row),
                 pl.BlockSpec((tm, S5_SEG), lambda i: (i % tiles_per_seq, i // tiles_per_seq))]
    out_shape = [jax.ShapeDtypeStruct((n_tok, SSD_SEG), F32),
                 jax.ShapeDtypeStruct((n_tok, ATT_SEG), F32),
                 jax.ShapeDtypeStruct((n_tok, RWKV_SEG), F32),
                 jax.ShapeDtypeStruct((seq, bsz * S5_SEG), F32)]
    if pending is None:
        return pl.pallas_call(
            _inproj_body, grid=(tiles,), in_specs=[pl.BlockSpec((tm, D_MODEL), row)] + const,
            out_specs=out_specs, out_shape=out_shape, compiler_params=_params(("parallel",)), name="inproj",
        )(x2, gain.reshape(1, D_MODEL), w_seg)
    y, route = pending
    return pl.pallas_call(
        _combine_inproj_body,
        grid=(tiles,),
        in_specs=[pl.BlockSpec((tm, D_MODEL), row), pl.BlockSpec((tm, D_MODEL), row),
                  pl.BlockSpec((tm, D_MODEL), lambda i: (tiles + i, 0)),
                  pl.BlockSpec((tm, ROUTE_LANES), row)] + const,
        out_specs=[pl.BlockSpec((tm, D_MODEL), row)] + out_specs,
        out_shape=[jax.ShapeDtypeStruct((n_tok, D_MODEL), F32)] + out_shape,
        compiler_params=_params(("parallel",)),
        name="combine_inproj",
    )(x2, y, y, route, gain.reshape(1, D_MODEL), w_seg)


def _segment_w_in(w):
    dt_cols = jnp.repeat(w[:, 768:772], HEAD_DIM, axis=1)
    return jnp.concatenate([w[:, :768], dt_cols, w[:, 772:]], axis=1).astype(BF16)


def _mix_out(x_ref, y_refs, wo_ref):
    acc = x_ref[...]
    for j, y_ref in enumerate(y_refs):
        acc = acc + jnp.dot(y_ref[...].astype(BF16), wo_ref[j * GROUP_WIDTH:(j + 1) * GROUP_WIDTH, :],
                            preferred_element_type=F32)
    return acc


def _mix_out_specs(tm, seq):
    tiles_per_seq = seq // tm
    row = lambda i: (i, 0)
    return [pl.BlockSpec((tm, D_MODEL), row),
            pl.BlockSpec((tm, GROUP_WIDTH), row),
            pl.BlockSpec((tm, GROUP_WIDTH), row),
            pl.BlockSpec((tm, GROUP_WIDTH), row),
            pl.BlockSpec((tm, GROUP_WIDTH), lambda i: (i % tiles_per_seq, i // tiles_per_seq))]


def _resident(shape):
    return pl.BlockSpec(shape, lambda i: (0,) * len(shape), pipeline_mode=pl.Buffered(1))


FFN_COLS = 256


def _outproj_ffn_body(x_ref, ya_ref, yb_ref, yc_ref, yd_ref, wo_ref, g_ref, wg_ref, wu_ref, wd_ref, o_ref,
                      xn_scr, act_scr):
    xn = _mix_out(x_ref, (ya_ref, yb_ref, yc_ref, yd_ref), wo_ref)
    xn_scr[...] = xn
    h = _rms(xn, g_ref[...]).astype(BF16)
    for j in range(D_FF // FFN_COLS):
        cols = slice(j * FFN_COLS, (j + 1) * FFN_COLS)
        gate = jnp.dot(h, wg_ref[:, cols], preferred_element_type=F32)
        up = jnp.dot(h, wu_ref[:, cols], preferred_element_type=F32)
        act_scr[:, cols] = (gate * jax.nn.sigmoid(gate) * up).astype(BF16)
    act = act_scr[...]
    for n in range(D_MODEL // FFN_COLS):
        cols = slice(n * FFN_COLS, (n + 1) * FFN_COLS)
        o_ref[:, cols] = xn_scr[:, cols] + jnp.dot(act, wd_ref[:, cols], preferred_element_type=F32)


def outproj_ffn(x2, ya, yb, yc, yd_tm, w_out_bf, gain, wg, wu, wd, seq, tm=512):
    n_tok = x2.shape[0]
    return pl.pallas_call(
        _outproj_ffn_body,
        grid=(n_tok // tm,),
        in_specs=_mix_out_specs(tm, seq) + [
            _resident((D_MODEL, D_MODEL)), pl.BlockSpec((1, D_MODEL), lambda i: (0, 0)),
            _resident((D_MODEL, D_FF)), _resident((D_MODEL, D_FF)), _resident((D_FF, D_MODEL))],
        out_specs=pl.BlockSpec((tm, D_MODEL), lambda i: (i, 0)),
        out_shape=jax.ShapeDtypeStruct((n_tok, D_MODEL), F32),
        scratch_shapes=[pltpu.VMEM((tm, D_MODEL), F32), pltpu.VMEM((tm, D_FF), BF16)],
        compiler_params=_params(("parallel",)),
        name="outproj_ffn",
    )(x2, ya, yb, yc, yd_tm, w_out_bf, gain.reshape(1, D_MODEL), wg, wu, wd)


S5_GROUPS = 16
S5_GROUP = 16
S5_STATE = 64
S5_CH = S5_GROUPS * S5_STATE
S5_BT = 8
S5_TIME_SEG = 32


def _s5_body(u_ref, wb_ref, wc_ref, are_ref, aim_ref, d_ref, gw_ref, gb_ref, gain_ref, o_ref, h_scr, st_scr):
    tt = u_ref.shape[0]

    @pl.when(pl.program_id(1) == 0)
    def _():
        st_scr[...] = jnp.zeros_like(st_scr)

    a_re = jnp.broadcast_to(are_ref[...], (S5_BT, S5_CH))
    a_im = jnp.broadcast_to(aim_ref[...], (S5_BT, S5_CH))
    seg = S5_TIME_SEG
    n_seg = tt // seg

    def inputs(q):
        return u_ref[q * seg:(q + 1) * seg].reshape(seg * S5_BT, GROUP_WIDTH)

    def project_in(q):
        bu = jnp.dot(inputs(q).astype(BF16), wb_ref[...], preferred_element_type=F32)
        h_scr[q * seg:(q + 1) * seg] = bu.reshape(seg, S5_BT, 2 * S5_CH)

    def scan(q, h_re, h_im):
        for t in range(q * seg, (q + 1) * seg):
            n_re = a_re * h_re - a_im * h_im + h_scr[t, :, :S5_CH]
            n_im = a_re * h_im + a_im * h_re + h_scr[t, :, S5_CH:]
            h_scr[t, :, :S5_CH] = n_re
            h_scr[t, :, S5_CH:] = n_im
            h_re, h_im = n_re, n_im
        return h_re, h_im

    def project_out(q):
        hall = h_scr[q * seg:(q + 1) * seg].reshape(seg * S5_BT, 2 * S5_CH)
        y = jnp.dot(hall.astype(BF16), wc_ref[...], preferred_element_type=F32) + d_ref[...] * inputs(q)
        hh = jax.nn.gelu(y)
        gl = jnp.dot(hh.astype(BF16), gw_ref[...], preferred_element_type=F32) + gb_ref[...]
        out = _rms(hh * jax.nn.sigmoid(gl), gain_ref[...])
        o_ref[q * seg:(q + 1) * seg] = out.reshape(seg, S5_BT, GROUP_WIDTH)

    h_re, h_im = st_scr[0], st_scr[1]
    project_in(0)
    for q in range(n_seg):
        if q + 1 < n_seg:
            project_in(q + 1)
        h_re, h_im = scan(q, h_re, h_im)
        project_out(q)
    st_scr[0] = h_re
    st_scr[1] = h_im


def _s5_weights(a_re, a_im, b_re, b_im, c_re, c_im, log_dt):
    dt = jnp.exp(log_dt)[:, None]
    mag = jnp.exp(dt * a_re)
    ab_re = mag * jnp.cos(dt * a_im)
    ab_im = mag * jnp.sin(dt * a_im)
    den = a_re * a_re + a_im * a_im
    f_re = ((ab_re - 1.0) * a_re + ab_im * a_im) / den
    f_im = (ab_im * a_re - (ab_re - 1.0) * a_im) / den
    bb_re = f_re[..., None] * b_re - f_im[..., None] * b_im
    bb_im = f_re[..., None] * b_im + f_im[..., None] * b_re
    eye = jnp.eye(S5_GROUPS, dtype=F32)
    expand_b = lambda m: jnp.einsum('gpc,gh->gchp', m, eye).reshape(GROUP_WIDTH, S5_CH)
    expand_c = lambda m: jnp.einsum('gcp,gh->gphc', m, eye).reshape(S5_CH, GROUP_WIDTH)
    wb = jnp.concatenate([expand_b(bb_re), expand_b(bb_im)], axis=1).astype(BF16)
    wc = jnp.concatenate([expand_c(c_re), -expand_c(c_im)], axis=0).astype(BF16)
    return wb, wc, ab_re.reshape(1, S5_CH), ab_im.reshape(1, S5_CH)


def s5_mixer(u_tm, a_re, a_im, b_re, b_im, c_re, c_im, log_dt, d_skip, glu_w, glu_b, out_gain, bsz, seq, tt=128):
    wb, wc, are, aim = _s5_weights(a_re, a_im, b_re, b_im, c_re, c_im, log_dt)
    u3 = u_tm.reshape(seq, bsz, GROUP_WIDTH)
    const = lambda shape: pl.BlockSpec(shape, lambda i, j: (0,) * len(shape))
    row = lambda v: v.reshape(1, GROUP_WIDTH)
    return pl.pallas_call(
        _s5_body,
        grid=(bsz // S5_BT, seq // tt),
        in_specs=[pl.BlockSpec((tt, S5_BT, GROUP_WIDTH), lambda i, j: (j, i, 0)),
                  const((GROUP_WIDTH, 2 * S5_CH)), const((2 * S5_CH, GROUP_WIDTH)),
                  const((1, S5_CH)), const((1, S5_CH)), const((1, GROUP_WIDTH)),
                  const((GROUP_WIDTH, GROUP_WIDTH)), const((1, GROUP_WIDTH)), const((1, GROUP_WIDTH))],
        out_specs=pl.BlockSpec((tt, S5_BT, GROUP_WIDTH), lambda i, j: (j, i, 0)),
        out_shape=jax.ShapeDtypeStruct((seq, bsz, GROUP_WIDTH), F32),
        scratch_shapes=[pltpu.VMEM((tt, S5_BT, 2 * S5_CH), F32), pltpu.VMEM((2, S5_BT, S5_CH), F32)],
        compiler_params=_params(("parallel", "arbitrary")),
        name="s5_mixer",
    )(u3, wb, wc, are, aim, row(d_skip), glu_w.astype(BF16), row(glu_b), row(out_gain))


SSD_CHUNK = 128
SSD_CONV = 4
SSD_HEADS = 4
SSD_TAIL = 8
HIGHEST = lax.Precision.HIGHEST


def _dot_nt(a, b, **kw):
    return lax.dot_general(a, b, (((1,), (1,)), ((), ())), preferred_element_type=F32, **kw)


def _dot_tn(a, b, **kw):
    return lax.dot_general(a, b, (((0,), (0,)), ((), ())), preferred_element_type=F32, **kw)


def _ssd_body(p_ref, cw_ref, cb_ref, dtb_ref, arow_ref, drow_ref, gain_ref, o_ref, tail_scr, st_scr):
    tb = p_ref.shape[0]
    L = SSD_CHUNK

    @pl.when(pl.program_id(1) == 0)
    def _():
        tail_scr[...] = jnp.zeros_like(tail_scr)
        st_scr[...] = jnp.zeros_like(st_scr)

    raw = p_ref[:, :SSD_XBC]
    ext = jnp.concatenate([tail_scr[...], raw], axis=0)
    conv = cb_ref[...] + cw_ref[SSD_CONV - 1:SSD_CONV, :] * raw
    for j in range(SSD_CONV - 1):
        shifted = pltpu.roll(ext, SSD_CONV - 1 - j, axis=0)[SSD_TAIL:, :]
        conv = conv + cw_ref[j:j + 1, :] * shifted
    tail_scr[...] = raw[tb - SSD_TAIL:, :]
    xc = conv * jax.nn.sigmoid(conv)

    dt = jax.nn.softplus(p_ref[:, 768:1024] + dtb_ref[...])
    a = dt * arow_ref[...]

    ii = lax.broadcasted_iota(jnp.int32, (L, L), 0)
    jj = lax.broadcasted_iota(jnp.int32, (L, L), 1)
    causal = ii >= jj
    tril = causal.astype(BF16)
    low_half = jj < HEAD_DIM
    lane2 = lax.broadcasted_iota(jnp.int32, (L, GROUP_WIDTH), 1)
    lane1 = lax.broadcasted_iota(jnp.int32, (L, L), 1)
    row_grp = lax.broadcasted_iota(jnp.int32, (L, GROUP_WIDTH), 0) // HEAD_DIM
    state_mask = row_grp == lane2 // (2 * HEAD_DIM)

    for c in range(tb // L):
        sl = slice(c * L, (c + 1) * L)
        xs = xc[sl, :GROUP_WIDTH]
        bm = xc[sl, GROUP_WIDTH:GROUP_WIDTH + L]
        cm = xc[sl, GROUP_WIDTH + L:]
        xdt = xs * dt[sl]
        a_hi = a[sl].astype(BF16)
        a_lo = (a[sl] - a_hi.astype(F32)).astype(BF16)
        cs = (jnp.dot(tril, a_hi, preferred_element_type=F32)
              + jnp.dot(tril, a_lo, preferred_element_type=F32))
        cs_end = cs[L - 1:L, :]
        st = st_scr[...]
        y = jnp.exp(cs) * jnp.dot(cm.astype(BF16), st.astype(BF16), preferred_element_type=F32)
        for g in range(2):
            cg = jnp.where((lane1 // HEAD_DIM) == g, cm, 0.0)
            gram = _dot_nt(cg.astype(BF16), bm.astype(BF16))
            pair = cs[:, g * L:(g + 1) * L]
            swapped = pltpu.roll(pair, HEAD_DIM, axis=1)
            for k in range(2):
                h = 2 * g + k
                col = jnp.where(low_half, pair, swapped) if k == 0 else jnp.where(low_half, swapped, pair)
                seg = col - col.T
                decay = jnp.where(causal, jnp.exp(jnp.minimum(seg, 0.0)), 0.0)
                yd = jnp.dot((gram * decay).astype(BF16), xdt.astype(BF16), preferred_element_type=F32)
                y = y + jnp.where((lane2 // HEAD_DIM) == h, yd, 0.0)
        upd = _dot_tn(bm.astype(BF16), (xdt * jnp.exp(cs_end - cs)).astype(BF16))
        st_scr[...] = st * jnp.exp(cs_end) + jnp.where(state_mask, upd, 0.0)
        y = y + drow_ref[...] * xs
        z = p_ref[sl, 512:768]
        o_ref[sl, :] = _rms(y * (z * jax.nn.sigmoid(z)), gain_ref[...])


def ssd_mixer(p_ssd, conv_w, conv_b, dt_bias, a_log, d_skip, norm_gain, bsz, seq, tb=512):
    n_tok = p_ssd.shape[0]
    blocks = seq // tb
    per_head = lambda v: jnp.repeat(v.astype(F32), HEAD_DIM).reshape(1, GROUP_WIDTH)
    const = lambda shape: pl.BlockSpec(shape, lambda b, j: (0,) * len(shape))
    return pl.pallas_call(
        _ssd_body,
        grid=(bsz, blocks),
        in_specs=[pl.BlockSpec((tb, SSD_SEG), lambda b, j: (b * blocks + j, 0)),
                  const((SSD_CONV, SSD_XBC)), const((1, SSD_XBC)), const((1, GROUP_WIDTH)),
                  const((1, GROUP_WIDTH)), const((1, GROUP_WIDTH)), const((1, GROUP_WIDTH))],
        out_specs=pl.BlockSpec((tb, GROUP_WIDTH), lambda b, j: (b * blocks + j, 0)),
        out_shape=jax.ShapeDtypeStruct((n_tok, GROUP_WIDTH), F32),
        scratch_shapes=[pltpu.VMEM((SSD_TAIL, SSD_XBC), F32), pltpu.VMEM((2 * HEAD_DIM, GROUP_WIDTH), F32)],
        compiler_params=_params(("parallel", "arbitrary")),
        name="ssd_mixer",
    )(p_ssd, conv_w, conv_b.reshape(1, SSD_XBC), per_head(dt_bias), per_head(-jnp.exp(a_log)), per_head(d_skip),
      norm_gain.reshape(1, GROUP_WIDTH))


ATT_HEADS = 4
ATT_BLOCK = 128
DILATED_PATTERNS = ((128, 1), (512, 4), (2048, 16))
MASKED = -1e30
ATT_PAR = 4


def _att_bias():
    slopes = [2.0 ** (-8.0 / ATT_HEADS * (h + 1)) for h in range(ATT_HEADS)]
    q_idx = jnp.arange(ATT_BLOCK)[:, None]
    k_idx = jnp.arange(2 * ATT_BLOCK)[None, :] - ATT_BLOCK
    rel = q_idx - k_idx
    out = []
    for window, dilation in DILATED_PATTERNS:
        span = window // dilation
        valid = (rel >= 0) & (rel <= span)
        per_head = [jnp.where(valid, -s * (rel * dilation).astype(F32), MASKED) for s in slopes]
        out.append(jnp.stack(per_head))
    return jnp.stack(out)


def _ld2(scr, rows):
    return jnp.concatenate([scr[0, rows, :], scr[1, rows, :]], axis=1)


def _st2(scr, rows, val):
    scr[0, rows, :] = val[:, :ATT_BLOCK]
    scr[1, rows, :] = val[:, ATT_BLOCK:]


def _att_body(qkv_ref, gmat_ref, qg_ref, kg_ref, og_ref, bias_ref, o_ref, q_scr, k_scr, v_scr, u_scr, m_scr, s_scr):
    seq = qkv_ref.shape[0]
    blk = ATT_BLOCK
    lane = lax.broadcasted_iota(jnp.int32, (blk, GROUP_WIDTH), 1) // HEAD_DIM
    every = pl.ds(0, seq)

    def head_norm(t, gain):
        sq = t * t
        hi = sq.astype(BF16)
        lo = (sq - hi.astype(F32)).astype(BF16)
        ms = (jnp.dot(hi, gmat_ref[...], preferred_element_type=F32)
              + jnp.dot(lo, gmat_ref[...], preferred_element_type=F32))
        return t * lax.rsqrt(ms + NORM_EPS) * gain

    _st2(q_scr, every, head_norm(qkv_ref[:, :GROUP_WIDTH], qg_ref[...]) * (1.0 / math.sqrt(HEAD_DIM)))
    _st2(k_scr, every, head_norm(qkv_ref[:, GROUP_WIDTH:2 * GROUP_WIDTH], kg_ref[...]))
    _st2(v_scr, every, qkv_ref[:, 2 * GROUP_WIDTH:])

    prev_cols = lax.broadcasted_iota(jnp.int32, (1, 2 * blk), 1) < blk
    heads = range(ATT_HEADS)

    def blocks(pat, specs):
        pairs = [(b, h) for b in range(len(specs)) for h in heads]
        q, keys, vals, gate = [], [], [], []
        for rows, prev_rows, has_prev in specs:
            q.append(_ld2(q_scr, rows))
            if prev_rows is None:
                keys.append(_ld2(k_scr, rows).astype(BF16))
                vals.append(_ld2(v_scr, rows).astype(BF16))
                gate.append(None)
            else:
                keys.append(jnp.concatenate([_ld2(k_scr, prev_rows), _ld2(k_scr, rows)], axis=0).astype(BF16))
                vals.append(jnp.concatenate([_ld2(v_scr, prev_rows), _ld2(v_scr, rows)], axis=0).astype(BF16))
                gate.append(jnp.where(jnp.logical_or(has_prev, jnp.logical_not(prev_cols)), 0.0, MASKED))
        bias = {(b, h): (bias_ref[pat, h, :, blk:] if gate[b] is None else bias_ref[pat, h] + gate[b])
                for b, h in pairs}
        qh = {(b, h): jnp.where(lane == h, q[b], 0.0).astype(BF16) for b, h in pairs}
        sc = {(b, h): _dot_nt(qh[b, h], keys[b]) + bias[b, h] for b, h in pairs}
        m = {bh: jnp.max(sc[bh], axis=-1, keepdims=True) for bh in pairs}
        p = {bh: jnp.exp(sc[bh] - m[bh]) for bh in pairs}
        den = {bh: jnp.sum(p[bh], axis=-1, keepdims=True) for bh in pairs}
        u = {(b, h): jnp.dot(p[b, h].astype(BF16), vals[b], preferred_element_type=F32) for b, h in pairs}
        out = []
        for b in range(len(specs)):
            u_all, m_all, s_all = u[b, 0], m[b, 0], den[b, 0]
            for h in heads[1:]:
                sel = lane == h
                u_all = jnp.where(sel, u[b, h], u_all)
                m_all = jnp.where(sel, m[b, h], m_all)
                s_all = jnp.where(sel, den[b, h], s_all)
            out.append((u_all, m_all, s_all))
        return out

    def merge(rows, u, m, s):
        m0 = _ld2(m_scr, rows)
        m_new = jnp.maximum(m0, m)
        alpha = jnp.exp(m0 - m_new)
        beta = jnp.exp(m - m_new)
        return alpha * _ld2(u_scr, rows) + beta * u, m_new, alpha * _ld2(s_scr, rows) + beta * s

    def save(rows, u, m, s):
        _st2(u_scr, rows, u)
        _st2(m_scr, rows, m)
        _st2(s_scr, rows, s)

    par = ATT_PAR
    n_iter = 16 // par

    def body16(i, _):
        specs = [(pl.ds(i + n_iter * b, blk, stride=16), None, None) for b in range(par)]
        for spec, res in zip(specs, blocks(2, specs)):
            save(spec[0], *res)
        return 0
    lax.fori_loop(0, n_iter, body16, 0)

    def body4(i, _):
        specs = []
        for b in range(par):
            idx = i + n_iter * b
            r = idx % 4
            n = idx // 4
            specs.append((pl.ds(n * (4 * blk) + r, blk, stride=4),
                          pl.ds(jnp.maximum(n - 1, 0) * (4 * blk) + r, blk, stride=4), n > 0))
        for spec, res in zip(specs, blocks(1, specs)):
            save(spec[0], *merge(spec[0], *res))
        return 0
    lax.fori_loop(0, n_iter, body4, 0)

    def body1(i, _):
        specs = []
        for b in range(par):
            n = i + (seq // blk // par) * b
            specs.append((pl.ds(pl.multiple_of(n * blk, blk), blk),
                          pl.ds(pl.multiple_of(jnp.maximum(n - 1, 0) * blk, blk), blk), n > 0))
        for spec, res in zip(specs, blocks(0, specs)):
            u, m, s = merge(spec[0], *res)
            o_ref[spec[0], :] = _rms(u / s, og_ref[...])
        return 0
    lax.fori_loop(0, seq // blk // par, body1, 0)


def attention_mixer(qkv, q_gain, k_gain, out_gain, bsz, seq):
    n_tok = qkv.shape[0]
    head_id = jnp.arange(GROUP_WIDTH) // HEAD_DIM
    gmat = ((head_id[:, None] == head_id[None, :]).astype(F32) / HEAD_DIM).astype(BF16)
    tile = lambda g: jnp.tile(g.astype(F32), ATT_HEADS).reshape(1, GROUP_WIDTH)
    const = lambda shape: pl.BlockSpec(shape, lambda b: (0,) * len(shape))
    return pl.pallas_call(
        _att_body,
        grid=(bsz,),
        in_specs=[pl.BlockSpec((seq, ATT_SEG), lambda b: (b, 0)),
                  const((GROUP_WIDTH, GROUP_WIDTH)), const((1, GROUP_WIDTH)), const((1, GROUP_WIDTH)),
                  const((1, GROUP_WIDTH)), const((3, ATT_HEADS, ATT_BLOCK, 2 * ATT_BLOCK))],
        out_specs=pl.BlockSpec((seq, GROUP_WIDTH), lambda b: (b, 0)),
        out_shape=jax.ShapeDtypeStruct((n_tok, GROUP_WIDTH), F32),
        scratch_shapes=[pltpu.VMEM((2, seq, ATT_BLOCK), F32) for _ in range(6)],
        compiler_params=_params(("parallel",)),
        name="attention_mixer",
    )(qkv, gmat, tile(q_gain), tile(k_gain), out_gain.reshape(1, GROUP_WIDTH), _att_bias())


RWKV_HEADS = 4
RWKV_CHUNK = 64
RWKV_DECAY_SCALE = 0.606531
RWKV_GN_EPS = 64e-5
RWKV_MAT = RWKV_HEADS * RWKV_CHUNK


def _per_head_blocks(x):
    lane_head = lax.broadcasted_iota(jnp.int32, x.shape, 1) // HEAD_DIM
    return jnp.concatenate([jnp.where(lane_head == h, x, 0.0) for h in range(RWKV_HEADS)], axis=0)


def _rwkv_body(p_ref, mu_ref, w0_ref, w2_ref, a0_ref, a2_ref, g2_ref, kk_ref, ka_ref, rk_ref, lnw_ref, lnb_ref,
               hsum_ref, o_ref, prev_scr, z_scr):
    nb, tb = p_ref.shape[0], p_ref.shape[1]
    L = RWKV_CHUNK

    def mb(a, b):
        return jnp.dot(a.astype(BF16), b.astype(BF16), preferred_element_type=F32)

    def split(x):
        hi = x.astype(BF16)
        return hi, (x - hi.astype(F32)).astype(BF16)

    def head_sum(x):
        return jnp.dot(x.astype(BF16), hsum_ref[...], preferred_element_type=F32)

    @pl.when(pl.program_id(1) == 0)
    def _():
        prev_scr[...] = jnp.zeros_like(prev_scr)
        z_scr[...] = jnp.zeros_like(z_scr)

    ii = lax.broadcasted_iota(jnp.int32, (L, RWKV_MAT), 0)
    jj = lax.broadcasted_iota(jnp.int32, (L, RWKV_MAT), 1) % L
    eye_all = (ii == jj).astype(F32)
    ti = lax.broadcasted_iota(jnp.int32, (L, L), 0)
    tj = lax.broadcasted_iota(jnp.int32, (L, L), 1)
    tril = (ti >= tj).astype(BF16)
    first_row = lax.broadcasted_iota(jnp.int32, (L, RWKV_SEG), 0) == 0
    zi = lax.broadcasted_iota(jnp.int32, (RWKV_MAT, RWKV_MAT), 0) // HEAD_DIM
    zj = lax.broadcasted_iota(jnp.int32, (RWKV_MAT, RWKV_MAT), 1) // HEAD_DIM
    same_head = zi == zj

    def each(fn, *lists):
        return [fn(*args) for args in zip(*lists)]

    def chunks(c, _):
        seqs = list(range(nb))
        rows = pl.ds(pl.multiple_of(c * L, L), L)
        pc = [p_ref[s, rows, :] for s in seqs]
        prev = [jnp.where(first_row, prev_scr[s], pltpu.roll(pc[s], 1, axis=0)) for s in seqs]
        for s in seqs:
            prev_scr[s] = pc[s][L - 1:L, :]
        pf = each(lambda x, xp: x + mu_ref[...] * (xp - x), pc, prev)
        r = [x[:, :GROUP_WIDTH] for x in pf]
        k = [x[:, GROUP_WIDTH:2 * GROUP_WIDTH] for x in pf]
        v = [x[:, 2 * GROUP_WIDTH:3 * GROUP_WIDTH] for x in pf]
        wa = [x[:, 3 * GROUP_WIDTH:3 * GROUP_WIDTH + 128] for x in pf]
        gd = [x[:, 3 * GROUP_WIDTH + 128:] for x in pf]
        log_w = each(lambda x: -RWKV_DECAY_SCALE * jax.nn.sigmoid(
            w0_ref[...] + jnp.dot(jnp.tanh(x).astype(BF16), w2_ref[...], preferred_element_type=F32)), wa)
        alr = each(lambda x: jax.nn.sigmoid(
            a0_ref[...] + jnp.dot(x.astype(BF16), a2_ref[...], preferred_element_type=F32)), wa)
        g = each(lambda x: jnp.dot(jax.nn.sigmoid(x).astype(BF16), g2_ref[...], preferred_element_type=F32), gd)
        kk = each(lambda x: x * kk_ref[...], k)
        kk_ss = each(lambda x: head_sum(x * x), kk)
        kk = each(lambda x, ss: x * lax.rsqrt(ss + 1e-12), kk, kk_ss)
        k2 = each(lambda x, a: x * (1.0 + (a - 1.0) * ka_ref[...]), k, alr)

        def cumsum(lw):
            hi, lo = split(lw)
            return (jnp.dot(tril, hi, preferred_element_type=F32) + jnp.dot(tril, lo, preferred_element_type=F32))
        cs = each(cumsum, log_w)
        gam = each(jnp.exp, cs)
        gam_inv = each(lambda x: jnp.exp(-x), cs)
        a_hat = each(lambda x, c_, lw: -x * jnp.exp(c_ - lw), kk, cs, log_w)
        b_hat = each(lambda x, a, gi: x * a * gi, kk, alr, gam_inv)
        k_hat = each(lambda x, gi: x * gi, k2, gam_inv)
        r_hat = each(lambda x, gm: x * gm, r, gam)

        gram = each(lambda a, rr, b, kh: _dot_nt(
            jnp.concatenate([a, rr], axis=0).astype(BF16),
            jnp.concatenate([_per_head_blocks(b), _per_head_blocks(kh)], axis=0).astype(BF16)),
            a_hat, r_hat, b_hat, k_hat)
        n_all = [jnp.where(ii > jj, x[:L, :RWKV_MAT], 0.0) for x in gram]
        m_all = [jnp.where(ii > jj, x[:L, RWKV_MAT:], 0.0) for x in gram]
        gb_all = [jnp.where(ii >= jj, x[L:, :RWKV_MAT], 0.0) for x in gram]
        gk_all = [jnp.where(ii >= jj, x[L:, RWKV_MAT:], 0.0) for x in gram]

        w_all = [eye_all + n for n in n_all]
        n_pow = each(lambda n: mb(n, _per_head_blocks(n)), n_all)
        for _ in range(4):
            res = each(lambda n, w: mb(jnp.concatenate([n, w], axis=0), _per_head_blocks(n)), n_pow, w_all)
            n_pow = [x[:L] for x in res]
            w_all = each(lambda w, x: w + x[L:], w_all, res)
        w_all = each(lambda w, n: w + mb(w, _per_head_blocks(n)), w_all, n_pow)

        t1 = each(lambda m, vv: mb(m, _per_head_blocks(vv)), m_all, v)
        at_vt = each(lambda w, a, t: mb(w, jnp.concatenate([_per_head_blocks(a), _per_head_blocks(t)], axis=1)),
                     w_all, a_hat, t1)
        a_til = [x[:, :RWKV_MAT] for x in at_vt]
        v_til = [x[:, RWKV_MAT:] for x in at_vt]

        z = [z_scr[s] for s in seqs]
        ax_rx = each(lambda a, rr, zz: _dot_nt(jnp.concatenate([a, rr], axis=0).astype(BF16), zz.astype(BF16)),
                     a_til, r_hat, z)
        u = each(lambda x, vt: x[:L] + vt, ax_rx, v_til)
        y = each(lambda x, gb, gk, uu, vv: x[L:] + mb(
            jnp.concatenate([gb, gk], axis=1),
            jnp.concatenate([_per_head_blocks(uu), _per_head_blocks(vv)], axis=0)), ax_rx, gb_all, gk_all, u, v)
        d = each(lambda uu, vv, b, kh: _dot_tn(jnp.concatenate([uu, vv], axis=0).astype(BF16),
                                               jnp.concatenate([b, kh], axis=0).astype(BF16)), u, v, b_hat, k_hat)
        for s in seqs:
            z_scr[s] = (z[s] + jnp.where(same_head, d[s], 0.0)) * gam[s][L - 1:L, :]

        mean = each(lambda x: head_sum(x) * (1.0 / HEAD_DIM), y)
        yc = each(lambda x, m: x - m, y, mean)
        var = each(lambda x: head_sum(x * x) * (1.0 / HEAD_DIM), yc)
        yn = each(lambda x, vr: x * lax.rsqrt(vr + RWKV_GN_EPS) * lnw_ref[...] + lnb_ref[...], yc, var)
        bonus = each(lambda rr, kx, vv: head_sum(rr * kx * rk_ref[...]) * vv, r, k2, v)
        for s in seqs:
            o_ref[s, rows, :] = (yn[s] + bonus[s]) * g[s]
        return 0

    lax.fori_loop(0, tb // L, chunks, 0)


def rwkv_mixer(p_rwkv, mu, w0, w2, a0, a2, g2, k_k, k_a, r_k, ln_w, ln_b, bsz, seq, tb=256, nb=8):
    n_tok = p_rwkv.shape[0]
    nb = nb if bsz % nb == 0 else 1
    row = lambda t: t.reshape(1, -1).astype(F32)
    w2p = jnp.concatenate([w2, jnp.zeros_like(a2)], axis=0).astype(BF16)
    a2p = jnp.concatenate([jnp.zeros_like(w2), a2], axis=0).astype(BF16)
    head_id = jnp.arange(GROUP_WIDTH) // HEAD_DIM
    hsum = (head_id[:, None] == head_id[None, :]).astype(BF16)
    const = lambda shape: pl.BlockSpec(shape, lambda b, j: (0,) * len(shape))
    vec = const((1, GROUP_WIDTH))
    return pl.pallas_call(
        _rwkv_body,
        grid=(bsz // nb, seq // tb),
        in_specs=[pl.BlockSpec((nb, tb, RWKV_SEG), lambda b, j: (b, j, 0)),
                  const((1, RWKV_SEG)), vec, const((128, GROUP_WIDTH)), vec, const((128, GROUP_WIDTH)),
                  const((128, GROUP_WIDTH)), vec, vec, vec, vec, vec, const((GROUP_WIDTH, GROUP_WIDTH))],
        out_specs=pl.BlockSpec((nb, tb, GROUP_WIDTH), lambda b, j: (b, j, 0)),
        out_shape=jax.ShapeDtypeStruct((bsz, seq, GROUP_WIDTH), F32),
        scratch_shapes=[pltpu.VMEM((nb, 1, RWKV_SEG), F32), pltpu.VMEM((nb, RWKV_MAT, RWKV_MAT), F32)],
        compiler_params=_params(("parallel", "arbitrary")),
        name="rwkv_mixer",
    )(p_rwkv.reshape(bsz, seq, RWKV_SEG), row(mu), row(w0), w2p, row(a0), a2p, g2.astype(BF16), row(k_k), row(k_a),
      row(r_k), row(ln_w), row(ln_b), hsum).reshape(n_tok, GROUP_WIDTH)


N_EXPERTS = 8
TOP_K = 2
MOE_ROWS = 256
ROUTE_LANES = 128
FF_CHUNK = 256
OUT_SLOTS = 3


def _outproj_router_body(x_ref, ya_ref, yb_ref, yc_ref, yd_ref, wo_ref, g_ref, w_ref, b_ref, xn_ref, o_ref):
    xn = _mix_out(x_ref, (ya_ref, yb_ref, yc_ref, yd_ref), wo_ref)
    xn_ref[...] = xn
    h = _rms(xn, g_ref[...])
    h_hi = h.astype(BF16)
    h_lo = (h - h_hi.astype(F32)).astype(BF16)
    both = jnp.dot(h_hi, w_ref[...], preferred_element_type=F32)
    logits = (both[:, :ROUTE_LANES] + both[:, ROUTE_LANES:]
              + jnp.dot(h_lo, w_ref[:, :ROUTE_LANES], preferred_element_type=F32) + b_ref[...])
    lane = lax.broadcasted_iota(jnp.int32, logits.shape, 1)
    m1 = jnp.max(logits, axis=-1, keepdims=True)
    i1 = jnp.min(jnp.where(logits == m1, lane, ROUTE_LANES), axis=-1, keepdims=True)
    rest = jnp.where(lane == i1, MASKED, logits)
    m2 = jnp.max(rest, axis=-1, keepdims=True)
    i2 = jnp.min(jnp.where(rest == m2, lane, ROUTE_LANES), axis=-1, keepdims=True)
    e2 = jnp.exp(m2 - m1)
    g1 = 1.0 / (1.0 + e2)
    out = jnp.where(lane == 0, i1.astype(F32), jnp.where(lane == 1, i2.astype(F32),
                    jnp.where(lane == 2, g1, jnp.where(lane == 3, e2 * g1, 0.0))))
    o_ref[...] = out


def outproj_router(x2, ya, yb, yc, yd_tm, w_out_bf, gain, router_w, router_b, seq, tm=512):
    n_tok = x2.shape[0]
    w = jnp.pad(router_w.astype(F32), ((0, 0), (0, ROUTE_LANES - N_EXPERTS)))
    w_hi = w.astype(BF16)
    w = jnp.concatenate([w_hi, (w - w_hi.astype(F32)).astype(BF16)], axis=1)
    b = jnp.pad(router_b.astype(F32), (0, ROUTE_LANES - N_EXPERTS), constant_values=MASKED).reshape(1, ROUTE_LANES)
    return pl.pallas_call(
        _outproj_router_body,
        grid=(n_tok // tm,),
        in_specs=_mix_out_specs(tm, seq) + [
            _resident((D_MODEL, D_MODEL)), pl.BlockSpec((1, D_MODEL), lambda i: (0, 0)),
            pl.BlockSpec((D_MODEL, 2 * ROUTE_LANES), lambda i: (0, 0)),
            pl.BlockSpec((1, ROUTE_LANES), lambda i: (0, 0))],
        out_specs=[pl.BlockSpec((tm, D_MODEL), lambda i: (i, 0)),
                   pl.BlockSpec((tm, ROUTE_LANES), lambda i: (i, 0))],
        out_shape=[jax.ShapeDtypeStruct((n_tok, D_MODEL), F32),
                   jax.ShapeDtypeStruct((n_tok, ROUTE_LANES), F32)],
        compiler_params=_params(("parallel",)),
        name="outproj_router",
    )(x2, ya, yb, yc, yd_tm, w_out_bf, gain.reshape(1, D_MODEL), w, b)


def _moe_dispatch(experts, n_tok):
    n_assign = n_tok * TOP_K
    n_blocks = n_assign // MOE_ROWS + N_EXPERTS
    flat_exp = experts.reshape(-1)
    order = jnp.argsort(flat_exp, stable=True).astype(jnp.int32)
    counts = jnp.sum((flat_exp[:, None] == jnp.arange(N_EXPERTS, dtype=jnp.int32)[None, :]).astype(jnp.int32), axis=0)
    start = jnp.cumsum(counts) - counts
    padded = (counts + MOE_ROWS - 1) // MOE_ROWS * MOE_ROWS
    pad_end = jnp.cumsum(padded)
    pad_start = pad_end - padded
    block_first = jnp.arange(n_blocks, dtype=jnp.int32) * MOE_ROWS
    block_exp = jnp.minimum(jnp.searchsorted(pad_end, block_first, side='right'), N_EXPERTS - 1).astype(jnp.int32)
    first_rank = block_first - pad_start[block_exp]
    row = jnp.arange(MOE_ROWS, dtype=jnp.int32)[None, :]
    valid = row < (counts[block_exp] - first_rank)[:, None]
    src = jnp.clip((start[block_exp] + first_rank)[:, None] + row, 0, n_assign - 1)
    slot_asg = jnp.where(valid, order[src], -1)
    slot_tok = jnp.where(valid, slot_asg // TOP_K, 0)
    spare = (n_assign + jnp.cumsum((~valid).reshape(-1).astype(jnp.int32)) - 1).reshape(n_blocks, MOE_ROWS)
    slot_dst = jnp.where(valid, (slot_asg % TOP_K) * n_tok + slot_asg // TOP_K, spare)
    return (slot_tok.reshape(n_blocks, 1, MOE_ROWS), slot_dst.reshape(n_blocks, 1, MOE_ROWS), block_exp)


def _expert_body(bexp_ref, tok0_ref, tok_next_ref, dst_prev_ref, dst_cur_ref, x_hbm, g_ref, wg_ref, wu_ref, wd_ref,
                 y_hbm, xbuf, obuf, act_scr, sem_in, sem_out):
    del bexp_ref
    i = pl.program_id(0)
    last = pl.num_programs(0) - 1
    cur = i % 2
    nxt = 1 - cur
    ocur = i % OUT_SLOTS
    oprev = (i + OUT_SLOTS - 1) % OUT_SLOTS
    n_chunks = D_FF // FF_CHUNK

    def gather_row(tok_ref, r, slot):
        return pltpu.make_async_copy(x_hbm.at[pl.ds(tok_ref[0, 0, r], 1)], xbuf.at[slot, pl.ds(r, 1)], sem_in.at[slot])

    def scatter_row(dst_ref, r, slot):
        return pltpu.make_async_copy(obuf.at[slot, pl.ds(r, 1)], y_hbm.at[pl.ds(dst_ref[0, 0, r], 1)], sem_out.at[slot])

    def wait_gather(slot):
        pltpu.make_async_copy(x_hbm.at[pl.ds(0, MOE_ROWS)], xbuf.at[slot], sem_in.at[slot]).wait()

    def wait_scatter(slot):
        pltpu.make_async_copy(obuf.at[slot], y_hbm.at[pl.ds(0, MOE_ROWS)], sem_out.at[slot]).wait()

    @pl.when(i == 0)
    def _():
        def start(r, _):
            gather_row(tok0_ref, r, 0).start()
            return 0
        lax.fori_loop(0, MOE_ROWS, start, 0)
        obuf[OUT_SLOTS - 1] = jnp.zeros((MOE_ROWS, D_MODEL), F32)

    wait_gather(cur)

    @pl.when(i >= OUT_SLOTS - 1)
    def _():
        wait_scatter(ocur)

    h = _rms(xbuf[cur], g_ref[...]).astype(BF16)
    out_chunks = D_MODEL // FF_CHUNK
    stages = n_chunks + out_chunks
    gather_stages = stages // 2
    rows_per_gather = -(-MOE_ROWS // gather_stages)
    rows_per_scatter = -(-MOE_ROWS // (stages - gather_stages))

    def start_rows(stage):
        if stage < gather_stages:
            for r in range(stage * rows_per_gather, min((stage + 1) * rows_per_gather, MOE_ROWS)):
                gather_row(tok_next_ref, r, nxt).start()
        else:
            st = stage - gather_stages
            for r in range(st * rows_per_scatter, min((st + 1) * rows_per_scatter, MOE_ROWS)):
                scatter_row(dst_prev_ref, r, oprev).start()

    def zero_after_starts():
        probe = pltpu.bitcast(xbuf[cur, 0:8, 0:FF_CHUNK], jnp.uint32)
        return pltpu.bitcast(lax.shift_right_logical(probe, jnp.uint32(32)), F32)[0:1, :]

    for j in range(n_chunks):
        start_rows(j)
        cols = slice(j * FF_CHUNK, (j + 1) * FF_CHUNK)
        gate = jnp.dot(h, wg_ref[0, :, cols], preferred_element_type=F32)
        up = jnp.dot(h, wu_ref[0, :, cols], preferred_element_type=F32)
        if j < gather_stages:
            up = up + zero_after_starts()
        act_scr[:, cols] = (gate * jax.nn.sigmoid(gate) * up).astype(BF16)

    act = act_scr[...]
    for n in range(out_chunks):
        start_rows(n_chunks + n)
        cols = slice(n * FF_CHUNK, (n + 1) * FF_CHUNK)
        obuf[ocur, :, cols] = jnp.dot(act, wd_ref[0, :, cols], preferred_element_type=F32)

    @pl.when(i == last)
    def _():
        def start(r, _):
            scatter_row(dst_cur_ref, r, ocur).start()
            return 0
        lax.fori_loop(0, MOE_ROWS, start, 0)
        for slot in range(OUT_SLOTS):
            wait_scatter(slot)
        wait_gather(nxt)


def moe_experts(x2, gain, slot_tok, slot_dst, block_exp, wg, wu, wd):
    n_blocks = block_exp.shape[0]
    cap = n_blocks * MOE_ROWS
    spare = (cap + jnp.arange(MOE_ROWS, dtype=jnp.int32)).reshape(1, 1, MOE_ROWS)
    dst_ext = jnp.concatenate([spare, slot_dst], axis=0)
    smem = lambda index: pl.BlockSpec((1, 1, MOE_ROWS), index, memory_space=pltpu.SMEM)
    grid_spec = pltpu.PrefetchScalarGridSpec(
        num_scalar_prefetch=1,
        grid=(n_blocks,),
        in_specs=[smem(lambda i, be: (0, 0, 0)),
                  smem(lambda i, be: (jnp.minimum(i + 1, n_blocks - 1), 0, 0)),
                  smem(lambda i, be: (i, 0, 0)),
                  smem(lambda i, be: (i + 1, 0, 0)),
                  pl.BlockSpec(memory_space=pl.ANY),
                  pl.BlockSpec((1, D_MODEL), lambda i, be: (0, 0)),
                  pl.BlockSpec((1, D_MODEL, D_FF), lambda i, be: (be[i], 0, 0)),
                  pl.BlockSpec((1, D_MODEL, D_FF), lambda i, be: (be[i], 0, 0)),
                  pl.BlockSpec((1, D_FF, D_MODEL), lambda i, be: (be[i], 0, 0))],
        out_specs=pl.BlockSpec(memory_space=pl.ANY),
        scratch_shapes=[pltpu.VMEM((2, MOE_ROWS, D_MODEL), F32), pltpu.VMEM((OUT_SLOTS, MOE_ROWS, D_MODEL), F32),
                        pltpu.VMEM((MOE_ROWS, D_FF), BF16),
                        pltpu.SemaphoreType.DMA((2,)), pltpu.SemaphoreType.DMA((OUT_SLOTS,))],
    )
    return pl.pallas_call(
        _expert_body,
        grid_spec=grid_spec,
        out_shape=jax.ShapeDtypeStruct((cap + MOE_ROWS, D_MODEL), F32),
        compiler_params=_params(("arbitrary",)),
        name="moe_experts",
    )(block_exp, slot_tok, slot_tok, dst_ext, dst_ext, x2, gain.reshape(1, D_MODEL), wg, wu, wd)


def _combine_body(x_ref, y1_ref, y2_ref, r_ref, o_ref):
    route = r_ref[...]
    o_ref[...] = x_ref[...] + route[:, 2:3] * y1_ref[...] + route[:, 3:4] * y2_ref[...]


def moe_combine(x2, y, route, tm=512):
    n_tok = x2.shape[0]
    tiles = n_tok // tm
    return pl.pallas_call(
        _combine_body,
        grid=(tiles,),
        in_specs=[pl.BlockSpec((tm, D_MODEL), lambda i: (i, 0)),
                  pl.BlockSpec((tm, D_MODEL), lambda i: (i, 0)),
                  pl.BlockSpec((tm, D_MODEL), lambda i: (tiles + i, 0)),
                  pl.BlockSpec((tm, ROUTE_LANES), lambda i: (i, 0))],
        out_specs=pl.BlockSpec((tm, D_MODEL), lambda i: (i, 0)),
        out_shape=jax.ShapeDtypeStruct((n_tok, D_MODEL), F32),
        compiler_params=_params(("parallel",)),
        name="moe_combine",
    )(x2, y, y, route)


def moe_ffn(x2, route, gain, wg, wu, wd):
    n_tok = x2.shape[0]
    experts = route[:, :TOP_K].astype(jnp.int32)
    slot_tok, slot_dst, block_exp = _moe_dispatch(experts, n_tok)
    return moe_experts(x2, gain, slot_tok, slot_dst, block_exp, wg, wu, wd)


def kernel(x, ln_mix, w_in, w_out, ssd_conv_w, ssd_conv_b, ssd_dt_bias, ssd_a_log, ssd_d, ssd_norm, att_q_norm, att_k_norm, att_out_norm, rwkv_mu, rwkv_w0, rwkv_w2, rwkv_a0, rwkv_a2, rwkv_g2, rwkv_k_k, rwkv_k_a, rwkv_r_k, rwkv_ln_w, rwkv_ln_b, s5_a_re, s5_a_im, s5_b_re, s5_b_im, s5_c_re, s5_c_im, s5_log_dt, s5_d, s5_glu_w, s5_glu_b, s5_out_norm, ln_ffn, ffn_w_gate, ffn_w_up, ffn_w_down, moe_router_w, moe_router_b, moe_w_gate, moe_w_up, moe_w_down):
    bsz, seq, dm = x.shape
    n_tok = bsz * seq
    depth = ln_mix.shape[0]
    x2 = x.reshape(n_tok, dm)
    pending = None
    for layer in range(depth):
        w_seg = _segment_w_in(w_in[layer])
        if pending is None:
            p_ssd, p_att, p_rwkv, p_s5 = inproj(x2, ln_mix[layer], w_seg, bsz, seq)
        else:
            x2, p_ssd, p_att, p_rwkv, p_s5 = inproj(x2, ln_mix[layer], w_seg, bsz, seq, pending)
            pending = None
        y_a = ssd_mixer(p_ssd, ssd_conv_w[layer], ssd_conv_b[layer], ssd_dt_bias[layer], ssd_a_log[layer],
                        ssd_d[layer], ssd_norm[layer], bsz, seq)
        y_b = attention_mixer(p_att, att_q_norm[layer], att_k_norm[layer], att_out_norm[layer], bsz, seq)
        y_c = rwkv_mixer(p_rwkv, rwkv_mu[layer], rwkv_w0[layer], rwkv_w2[layer], rwkv_a0[layer], rwkv_a2[layer],
                         rwkv_g2[layer], rwkv_k_k[layer], rwkv_k_a[layer], rwkv_r_k[layer], rwkv_ln_w[layer],
                         rwkv_ln_b[layer], bsz, seq)
        y_d = s5_mixer(p_s5, s5_a_re[layer], s5_a_im[layer], s5_b_re[layer], s5_b_im[layer], s5_c_re[layer],
                       s5_c_im[layer], s5_log_dt[layer], s5_d[layer], s5_glu_w[layer], s5_glu_b[layer],
                       s5_out_norm[layer], bsz, seq)
        y_d_tm = y_d.reshape(seq, bsz * GROUP_WIDTH)
        mixed = (x2, y_a, y_b, y_c, y_d_tm, w_out[layer].astype(BF16))
        idx = layer // 2
        if layer % 2 == 0:
            x2 = outproj_ffn(*mixed, ln_ffn[layer], ffn_w_gate[idx].astype(BF16), ffn_w_up[idx].astype(BF16),
                             ffn_w_down[idx].astype(BF16), seq)
        else:
            x2, route = outproj_router(*mixed, ln_ffn[layer], moe_router_w[idx], moe_router_b[idx], seq)
            y = moe_ffn(x2, route, ln_ffn[layer], moe_w_gate[idx].astype(BF16), moe_w_up[idx].astype(BF16),
                        moe_w_down[idx].astype(BF16))
            if layer + 1 < depth:
                pending = (y, route)
            else:
                x2 = moe_combine(x2, y, route)
    return x2.reshape(bsz, seq, dm)
```
